```python
import jax, jax.numpy as jnp
from jax import lax
import numpy as np

D_MODEL = 1024
BATCH = 16
SEQ = 4096
DEPTH = 2
DEC_BATCH = 8
DEC_SEQ = 16
PAST_LEN = 2048

CHUNK = 64
GMLP_CHUNK = 128
D_A = D_MODEL
A_GROUPS = 8
A_GROUP_W = D_A // A_GROUPS
B_HEADS = 8
B_DK = 128
B_DV = 128
D_B = B_HEADS * B_DV
CONV_W = 4
D_FF = 4 * D_MODEL
EPS = 1e-6
SPLIT_SIZES = (D_A, D_A, 3 * D_B, D_B, B_HEADS, B_HEADS, D_MODEL, D_MODEL)
N_IN = sum(SPLIT_SIZES)

kernel_name = "gmlp_gated_deltanet_streaming_encoder"


def rmsnorm(x, g):
    xf = x.astype(jnp.float32)
    y = xf * lax.rsqrt(jnp.mean(xf * xf, axis=-1, keepdims=True) + EPS)
    return (y * g.astype(jnp.float32)).astype(x.dtype)


def layernorm(x, g, b):
    xf = x.astype(jnp.float32)
    mu = jnp.mean(xf, axis=-1, keepdims=True)
    var = jnp.mean(jnp.square(xf - mu), axis=-1, keepdims=True)
    y = (xf - mu) * lax.rsqrt(var + EPS)
    return (y * g.astype(jnp.float32) + b.astype(jnp.float32)).astype(x.dtype)


def l2norm(x):
    return x * lax.rsqrt(jnp.sum(x * x, axis=-1, keepdims=True) + EPS)


def causal_conv(x, prev, w):
    L = x.shape[1]
    xp = jnp.concatenate([prev.astype(x.dtype), x], axis=1)
    y = xp[:, 0:L] * w[:, 0]
    for j in range(1, CONV_W):
        y = y + xp[:, j:j + L] * w[:, j]
    return y, xp[:, -(CONV_W - 1):]


def gmlp_spatial(vn, w_s, b_s):
    B, L, _ = vn.shape
    c = min(L, GMLP_CHUNK)
    n = L // c
    mask = jnp.tril(jnp.ones((c, c), dtype=bool))
    w = jnp.where(mask, w_s[:, :c, :c], 0.0).astype(vn.dtype)
    v5 = vn.reshape(B, n, c, A_GROUPS, A_GROUP_W)
    s = jnp.einsum('gts,bnsgc->bntgc', w, v5)
    s = s + jnp.transpose(b_s[:, :c])[None, None, :, :, None].astype(vn.dtype)
    return s.reshape(B, L, D_A)


def gated_delta_rule(q, k, v, g, beta, s0):
    B, L, H, _ = q.shape
    c = min(L, CHUNK)
    n = L // c
    qb = q.reshape(B, n, c, H, B_DK).transpose(1, 0, 3, 2, 4)
    kb_ = k.reshape(B, n, c, H, B_DK).transpose(1, 0, 3, 2, 4)
    vb = v.reshape(B, n, c, H, B_DV).transpose(1, 0, 3, 2, 4)
    gb = g.reshape(B, n, c, H).transpose(1, 0, 3, 2)
    bb = beta.reshape(B, n, c, H).transpose(1, 0, 3, 2)
    gam = jnp.cumsum(gb, axis=-1)
    causal = jnp.tril(jnp.ones((c, c), dtype=bool))
    strict = jnp.tril(jnp.ones((c, c), dtype=bool), -1)
    decay = jnp.exp(jnp.where(causal, gam[..., :, None] - gam[..., None, :], -jnp.inf))
    k_beta = kb_ * bb[..., None]
    A = jnp.where(strict, jnp.einsum('nbhid,nbhjd->nbhij', k_beta, kb_) * decay, 0.0)
    T = A + jnp.eye(c, dtype=A.dtype)
    U = lax.linalg.triangular_solve(T, vb * bb[..., None], left_side=True, lower=True, unit_diagonal=True)
    W = lax.linalg.triangular_solve(T, k_beta * jnp.exp(gam)[..., None], left_side=True, lower=True, unit_diagonal=True)
    P = jnp.einsum('nbhid,nbhjd->nbhij', qb, kb_) * decay
    qg = qb * jnp.exp(gam)[..., None]
    kg = kb_ * jnp.exp(gam[..., -1:] - gam)[..., None]
    glast = jnp.exp(gam[..., -1])

    def step(S, xs):
        U_, W_, P_, qg_, kg_, gl_ = xs
        vnew = U_ - jnp.einsum('bhcd,bhde->bhce', W_, S)
        o = jnp.einsum('bhcd,bhde->bhce', qg_, S) + jnp.einsum('bhij,bhje->bhie', P_, vnew)
        S = S * gl_[..., None, None] + jnp.einsum('bhcd,bhce->bhde', kg_, vnew)
        return S, o

    S, o = lax.scan(step, s0, (U, W, P, qg, kg, glast))
    o = o.transpose(1, 0, 3, 2, 4).reshape(B, L, H, B_DV)
    return o, S


def trunk_layer(x, conv_prev, s0, ln1, w_in, a_ln_g, a_ln_b, w_s, b_s, conv_w, a_log, dt_bias,
                o_norm, p_a, p_b, w_o, ln2, w_up, w_down):
    B, L, _ = x.shape
    h = rmsnorm(x, ln1)
    proj = h @ w_in
    idx = list(np.cumsum(SPLIT_SIZES)[:-1])
    u, va, qkv, z, b_raw, a_raw, ga, gb = jnp.split(proj, idx, axis=-1)

    u = jax.nn.gelu(u, approximate=False)
    vn = layernorm(jax.nn.gelu(va, approximate=False), a_ln_g, a_ln_b)
    z_a = u * gmlp_spatial(vn, w_s, b_s)

    qkv, conv_state = causal_conv(qkv, conv_prev, conv_w)
    qkv = jax.nn.silu(qkv).astype(jnp.float32)
    q, k, v = jnp.split(qkv, 3, axis=-1)
    q = l2norm(q.reshape(B, L, B_HEADS, B_DK)) * (B_DK ** -0.5)
    k = l2norm(k.reshape(B, L, B_HEADS, B_DK))
    v = v.reshape(B, L, B_HEADS, B_DV)
    beta = jax.nn.sigmoid(b_raw.astype(jnp.float32))
    g = -jnp.exp(a_log.astype(jnp.float32)) * jax.nn.softplus(a_raw.astype(jnp.float32) + dt_bias.astype(jnp.float32))
    o, S = gated_delta_rule(q, k, v, g, beta, s0.astype(jnp.float32))
    o = rmsnorm(o, o_norm) * jax.nn.silu(z.astype(jnp.float32).reshape(B, L, B_HEADS, B_DV))
    z_b = o.reshape(B, L, D_B).astype(x.dtype)

    mix = jax.nn.sigmoid(ga) * (z_a @ p_a) + jax.nn.sigmoid(gb) * (z_b @ p_b)
    x = x + mix @ w_o
    h2 = rmsnorm(x, ln2)
    x = x + jnp.square(jax.nn.relu(h2 @ w_up)) @ w_down
    return x, conv_state, S.astype(x.dtype), vn


def setup_inputs(seed: int = 0) -> dict:
    key = jax.random.key(seed)
    ks = jax.random.split(key, 24)
    f32 = jnp.float32
    nrm = lambda k, shape, s: jax.random.normal(k, shape, f32) * s
    dt = jnp.exp(jax.random.uniform(ks[13], (DEPTH, B_HEADS), f32, np.log(1e-3), np.log(1e-1)))
    return {
        "x_prompt": nrm(ks[0], (BATCH, SEQ, D_MODEL), 1.0),
        "x_sample": nrm(ks[1], (DEC_BATCH, DEC_SEQ, D_MODEL), 1.0),
        "state_conv": nrm(ks[2], (DEPTH, DEC_BATCH, CONV_W - 1, 3 * D_B), 1.0),
        "state_delta": nrm(ks[3], (DEPTH, DEC_BATCH, B_HEADS, B_DK, B_DV), 0.1),
        "ln1": 1.0 + nrm(ks[4], (DEPTH, D_MODEL), 0.02),
        "w_in": nrm(ks[5], (DEPTH, D_MODEL, N_IN), D_MODEL ** -0.5),
        "a_ln_g": 1.0 + nrm(ks[6], (DEPTH, D_A), 0.02),
        "a_ln_b": nrm(ks[7], (DEPTH, D_A), 0.02),
        "w_s": nrm(ks[8], (DEPTH, A_GROUPS, GMLP_CHUNK, GMLP_CHUNK), 0.5 * GMLP_CHUNK ** -0.5),
        "b_s": 1.0 + nrm(ks[9], (DEPTH, A_GROUPS, GMLP_CHUNK), 0.02),
        "conv_w": nrm(ks[10], (DEPTH, 3 * D_B, CONV_W), 0.5),
        "a_log": jnp.log(jax.random.uniform(ks[11], (DEPTH, B_HEADS), f32, 1.0, 16.0)),
        "dt_bias": dt + jnp.log(-jnp.expm1(-dt)),
        "o_norm": 1.0 + nrm(ks[12], (DEPTH, B_DV), 0.02),
        "p_a": nrm(ks[14], (DEPTH, D_A, D_MODEL), D_A ** -0.5),
        "p_b": nrm(ks[15], (DEPTH, D_B, D_MODEL), D_B ** -0.5),
        "w_o": nrm(ks[16], (DEPTH, D_MODEL, D_MODEL), D_MODEL ** -0.5),
        "ln2": 1.0 + nrm(ks[17], (DEPTH, D_MODEL), 0.02),
        "w_up": nrm(ks[18], (DEPTH, D_MODEL, D_FF), D_MODEL ** -0.5),
        "w_down": nrm(ks[19], (DEPTH, D_FF, D_MODEL), 0.5 * D_FF ** -0.5),
        "final_norm": 1.0 + nrm(ks[20], (D_MODEL,), 0.02),
    }


def reference(x_prompt, x_sample, state_conv, state_delta, ln1, w_in, a_ln_g, a_ln_b, w_s, b_s,
              conv_w, a_log, dt_bias, o_norm, p_a, p_b, w_o, ln2, w_up, w_down, final_norm):
    yp, ys = x_prompt, x_sample
    conv_p, delta_p, conv_s, delta_s, gv_s = [], [], [], [], []
    zero_conv = jnp.zeros((x_prompt.shape[0], CONV_W - 1, 3 * D_B), x_prompt.dtype)
    zero_delta = jnp.zeros((x_prompt.shape[0], B_HEADS, B_DK, B_DV), jnp.float32)
    for l in range(DEPTH):
        params = (ln1[l], w_in[l], a_ln_g[l], a_ln_b[l], w_s[l], b_s[l], conv_w[l], a_log[l], dt_bias[l],
                  o_norm[l], p_a[l], p_b[l], w_o[l], ln2[l], w_up[l], w_down[l])
        yp, cp, dp, _ = trunk_layer(yp, zero_conv, zero_delta, *params)
        ys, cs, ds, vs = trunk_layer(ys, state_conv[l], state_delta[l], *params)
        conv_p.append(cp); delta_p.append(dp); conv_s.append(cs); delta_s.append(ds); gv_s.append(vs)
    y_prompt = rmsnorm(yp, final_norm)
    y_sample = rmsnorm(ys, final_norm)
    new_conv_prompt = jnp.stack(conv_p)
    new_delta_prompt = jnp.stack(delta_p)
    new_conv_sample = jnp.stack(conv_s)
    new_delta_sample = jnp.stack(delta_s)
    new_gmlp_v_sample = jnp.stack(gv_s)
    return (y_prompt, y_sample, new_conv_prompt, new_delta_prompt, new_conv_sample, new_delta_sample, new_gmlp_v_sample)
```

```python
import functools

import jax
import jax.numpy as jnp
from jax import lax
from jax.experimental import pallas as pl
from jax.experimental.pallas import tpu as pltpu

F32 = jnp.float32
BF16 = jnp.bfloat16

D_MODEL = 1024
D_A = 1024
A_GROUPS = 8
A_GROUP_W = D_A // A_GROUPS
B_HEADS = 8
B_DK = 128
B_DV = 128
D_B = B_HEADS * B_DV
CONV_W = 4
D_FF = 4 * D_MODEL
EPS = 1e-6
GMLP_CHUNK = 128

LANES = 128
SUBLANES = 8
BLK = 128
BA_COLS = LANES
A_COL0 = B_HEADS
NEG_BIG = -1e30
VMEM_LIMIT = 60000 * 1024

TM_INPROJ = 256
TM_FFN = 512
FFN_CHUNK = 1024
TS_MIXER = 256


def _rms(x, g):
    return x * lax.rsqrt(jnp.mean(x * x, axis=-1, keepdims=True) + EPS) * g


def _gelu(x):
    return 0.5 * x * (1.0 + lax.erf(x * 0.7071067811865476))


def _silu(x):
    return x * jax.nn.sigmoid(x)


def _dot(a, b):
    return jnp.dot(a, b, preferred_element_type=F32)


def _dot_nt(a, b):
    return lax.dot_general(a, b, (((1,), (1,)), ((), ())), preferred_element_type=F32)


def _split3(v):
    hi = v.astype(BF16)
    r1 = v - hi.astype(F32)
    mid = r1.astype(BF16)
    lo = (r1 - mid.astype(F32)).astype(BF16)
    return hi, mid, lo


def _inproj_kernel(x_ref, ln1_ref, wuv_ref, wqkv_ref, wz_ref, wba_ref, wg_ref, alg_ref, alb_ref,
                   ug_ref, vn_ref, qkv_ref, sz_ref, ba_ref, sga_ref, sgb_ref):
    h = _rms(x_ref[...], ln1_ref[...]).astype(BF16)
    uv = _dot(h, wuv_ref[...])
    ug_ref[...] = _gelu(uv[:, :D_A])
    va = _gelu(uv[:, D_A:])
    mu = jnp.mean(va, axis=-1, keepdims=True)
    vc = va - mu
    var = jnp.mean(vc * vc, axis=-1, keepdims=True)
    vn_ref[...] = vc * lax.rsqrt(var + EPS) * alg_ref[...] + alb_ref[...]
    qkv_ref[...] = _dot(h, wqkv_ref[...])
    sz_ref[...] = _silu(_dot(h, wz_ref[...]))
    ba_ref[...] = _dot(h, wba_ref[...])
    gates = jax.nn.sigmoid(_dot(h, wg_ref[...]))
    sga_ref[...] = gates[:, :D_MODEL]
    sgb_ref[...] = gates[:, D_MODEL:]


def _inproj(x2, ln1, wuv, wqkv, wz, wba, wg, alg, alb):
    t = x2.shape[0]
    tm = min(TM_INPROJ, t)
    row = lambda c: pl.BlockSpec((tm, c), lambda i: (i, 0))
    const = lambda a: pl.BlockSpec(a.shape, lambda i: (0, 0), pipeline_mode=pl.Buffered(1))
    widths = (D_A, D_A, 3 * D_B, D_B, BA_COLS, D_MODEL, D_MODEL)
    return pl.pallas_call(
        _inproj_kernel,
        out_shape=[jax.ShapeDtypeStruct((t, c), F32) for c in widths],
        grid=(t // tm,),
        in_specs=[row(D_MODEL)] + [const(a) for a in (ln1, wuv, wqkv, wz, wba, wg, alg, alb)],
        out_specs=[row(c) for c in widths],
        compiler_params=pltpu.CompilerParams(dimension_semantics=("arbitrary",), vmem_limit_bytes=VMEM_LIMIT),
        name="inproj",
    )(x2, ln1, wuv, wqkv, wz, wba, wg, alg, alb)


def _level_masks():
    row = lax.broadcasted_iota(jnp.int32, (BLK, BLK), 0)
    col = lax.broadcasted_iota(jnp.int32, (BLK, BLK), 1)
    x = row ^ col
    masks = []
    b, k = 1, 0
    while b < BLK:
        masks.append(jnp.where(((x >> k) == 1) & ((row & b) != 0), 1.0, 0.0).astype(F32))
        b, k = 2 * b, k + 1
    return row, col, masks


def _unit_lower_inverse(a, eye, masks):
    x = eye - a * masks[0]
    for m in masks[1:]:
        xb = x.astype(BF16)
        y = _dot(xb, (a * m).astype(BF16))
        x = x - _dot(y.astype(BF16), xb)
    return x


def _mixer_kernel(ug_ref, vn_ref, qkv_ref, sz_ref, ba_ref, sga_ref, sgb_ref, x_ref,
                  conv0_ref, s0_ref, ws_ref, bst_ref, cw_ref, alog_ref, dtb_ref, on_ref,
                  pa_ref, pb_ref, wo_ref,
                  y_ref, convo_ref, so_ref,
                  xpad_ref, s_ref, za_ref, zb_ref, *, ts, valid):
    step = pl.program_id(1)
    nblk = ts // BLK

    @pl.when(step == 0)
    def _():
        xpad_ref[0:SUBLANES, :] = jnp.zeros((SUBLANES, 3 * D_B), F32)
        xpad_ref[SUBLANES - (CONV_W - 1):SUBLANES, :] = conv0_ref[...]
        s_ref[...] = s0_ref[...]

    xpad_ref[SUBLANES:SUBLANES + ts, :] = qkv_ref[...]
    convo_ref[...] = xpad_ref[SUBLANES + valid - (CONV_W - 1):SUBLANES + valid, :]

    def conv_slab(t0, c0):
        acc = None
        for j in range(CONV_W):
            r0 = t0 + SUBLANES - (CONV_W - 1) + j
            term = xpad_ref[r0:r0 + BLK, c0:c0 + LANES] * cw_ref[j:j + 1, c0:c0 + LANES]
            acc = term if acc is None else acc + term
        return _silu(acc)

    row, col, masks = _level_masks()
    causal = row >= col
    strict = row > col
    eye = jnp.where(row == col, 1.0, 0.0).astype(F32)
    tri01 = jnp.where(causal, 1.0, 0.0).astype(BF16)

    gchunk = min(GMLP_CHUNK, ts)
    for c in range(ts // gchunk):
        r0 = c * gchunk
        for g in range(A_GROUPS):
            c0 = g * A_GROUP_W
            w = jnp.where(causal, ws_ref[g], 0.0).astype(BF16)
            s = _dot(w, vn_ref[r0:r0 + gchunk, c0:c0 + A_GROUP_W].astype(BF16)) + bst_ref[:, g:g + 1]
            za_ref[r0:r0 + gchunk, c0:c0 + A_GROUP_W] = (ug_ref[r0:r0 + gchunk, c0:c0 + A_GROUP_W] * s).astype(BF16)

    blocks = []
    for c in range(nblk):
        r0 = c * BLK
        ba = ba_ref[r0:r0 + BLK, :]
        beta_c = jax.nn.sigmoid(ba)
        g_c = -jnp.exp(alog_ref[...]) * jax.nn.softplus(ba + dtb_ref[...])
        if valid < ts:
            live = lax.broadcasted_iota(jnp.int32, (BLK, BA_COLS), 0) + r0 < valid
            beta_c = jnp.where(live, beta_c, 0.0)
            g_c = jnp.where(live, g_c, 0.0)
        gam_c = sum(_dot(tri01, p) for p in _split3(g_c))
        gam_t = gam_c.T
        heads = []
        for h in range(B_HEADS):
            hc = h * B_DK
            q = conv_slab(r0, hc)
            k = conv_slab(r0, D_B + hc)
            v = conv_slab(r0, 2 * D_B + hc)
            q = q * lax.rsqrt(jnp.sum(q * q, axis=-1, keepdims=True) + EPS) * (B_DK ** -0.5)
            k = k * lax.rsqrt(jnp.sum(k * k, axis=-1, keepdims=True) + EPS)
            beta = beta_c[:, h:h + 1]
            gam = jnp.broadcast_to(gam_c[:, A_COL0 + h:A_COL0 + h + 1], (BLK, LANES))
            gam_row = gam_t[A_COL0 + h:A_COL0 + h + 1, :]
            gam_last = gam[BLK - 1:BLK, :]
            decay = jnp.exp(jnp.where(causal, gam - gam_row, NEG_BIG))
            kb = k * beta
            k16 = k.astype(BF16)
            a = jnp.where(strict, _dot_nt(kb.astype(BF16), k16) * decay, 0.0)
            p = _dot_nt(q.astype(BF16), k16) * decay
            tinv = _unit_lower_inverse(a, eye, masks)
            eg = jnp.exp(gam)
            rhs = jnp.concatenate([v * beta, kb * eg], axis=1).astype(BF16)
            uw = _dot(tinv.astype(BF16), rhs)
            heads.append(dict(
                u=uw[:, :B_DV], w=uw[:, B_DV:].astype(BF16), p=p.astype(BF16),
                qg=(q * eg).astype(BF16), kgt=(k * jnp.exp(gam_last - gam)).T.astype(BF16),
                gl=jnp.exp(gam_last)))
        blocks.append(heads)

    for c in range(nblk):
        r0 = c * BLK
        for h in range(B_HEADS):
            d = blocks[c][h]
            s_old = s_ref[h]
            s16 = s_old.astype(BF16)
            vnew = d["u"] - _dot(d["w"], s16)
            v16 = vnew.astype(BF16)
            o = _dot(d["qg"], s16) + _dot(d["p"], v16)
            s_ref[h] = s_old * d["gl"] + _dot(d["kgt"], v16)
            o = o * lax.rsqrt(jnp.mean(o * o, axis=-1, keepdims=True) + EPS) * on_ref[...]
            hc = h * B_DV
            zb_ref[r0:r0 + BLK, hc:hc + B_DV] = (o * sz_ref[r0:r0 + BLK, hc:hc + B_DV]).astype(BF16)

    xpad_ref[0:SUBLANES, :] = xpad_ref[ts:ts + SUBLANES, :]

    @pl.when(step == pl.num_programs(1) - 1)
    def _():
        so_ref[...] = s_ref[...]

    mix = sga_ref[...] * _dot(za_ref[...], pa_ref[...]) + sgb_ref[...] * _dot(zb_ref[...], pb_ref[...])
    y_ref[...] = x_ref[...] + _dot(mix.astype(BF16), wo_ref[...])


def _mixer(ug, vn, qkv, sz, ba, sga, sgb, x, conv0, s0, ws, bst, cw, alog, dtb, onorm, pa, pb, wo, *, valid):
    b, l, _ = x.shape
    ts = min(TS_MIXER, l)
    assert l % ts == 0 and ts % BLK == 0
    assert valid == ts or l == ts
    tok = lambda c: pl.BlockSpec((None, ts, c), lambda i, j: (i, j, 0))
    const = lambda a: pl.BlockSpec(a.shape, lambda i, j: (0,) * a.ndim, pipeline_mode=pl.Buffered(1))
    per_seq = lambda a: pl.BlockSpec((None,) + a.shape[1:], lambda i, j: (i,) + (0,) * (a.ndim - 1))
    consts = (ws, bst, cw, alog, dtb, onorm, pa, pb, wo)
    return pl.pallas_call(
        functools.partial(_mixer_kernel, ts=ts, valid=valid),
        out_shape=[jax.ShapeDtypeStruct((b, l, D_MODEL), F32),
                   jax.ShapeDtypeStruct((b, CONV_W - 1, 3 * D_B), F32),
                   jax.ShapeDtypeStruct((b, B_HEADS, B_DK, B_DV), F32)],
        grid=(b, l // ts),
        in_specs=[tok(D_A), tok(D_A), tok(3 * D_B), tok(D_B), tok(BA_COLS), tok(D_MODEL), tok(D_MODEL), tok(D_MODEL),
                  per_seq(conv0), per_seq(s0)] + [const(a) for a in consts],
        out_specs=[tok(D_MODEL),
                   pl.BlockSpec((None, CONV_W - 1, 3 * D_B), lambda i, j: (i, 0, 0)),
                   pl.BlockSpec((None, B_HEADS, B_DK, B_DV), lambda i, j: (i, 0, 0, 0))],
        scratch_shapes=[pltpu.VMEM((ts + SUBLANES, 3 * D_B), F32),
                        pltpu.VMEM((B_HEADS, B_DK, B_DV), F32),
                        pltpu.VMEM((ts, D_A), BF16),
                        pltpu.VMEM((ts, D_B), BF16)],
        compiler_params=pltpu.CompilerParams(dimension_semantics=("arbitrary", "arbitrary"),
                                             vmem_limit_bytes=VMEM_LIMIT),
        name="mixer",
    )(ug, vn, qkv, sz, ba, sga, sgb, x, conv0, s0, *consts)


def _ffn_kernel(x_ref, ln2_ref, wup_ref, wdn_ref, fn_ref, o_ref, *, final):
    x = x_ref[...]
    h = _rms(x, ln2_ref[...]).astype(BF16)
    acc = x
    for c in range(D_FF // FFN_CHUNK):
        c0 = c * FFN_CHUNK
        hid = _dot(h, wup_ref[:, c0:c0 + FFN_CHUNK])
        act = jnp.square(jnp.maximum(hid, 0.0)).astype(BF16)
        acc = acc + _dot(act, wdn_ref[c0:c0 + FFN_CHUNK, :])
    o_ref[...] = _rms(acc, fn_ref[...]) if final else acc


def _ffn(x2, ln2, wup, wdn, fnorm, *, final):
    t = x2.shape[0]
    tm = min(TM_FFN, t)
    row = pl.BlockSpec((tm, D_MODEL), lambda i: (i, 0))
    const = lambda a: pl.BlockSpec(a.shape, lambda i: (0, 0), pipeline_mode=pl.Buffered(1))
    return pl.pallas_call(
        functools.partial(_ffn_kernel, final=final),
        out_shape=jax.ShapeDtypeStruct((t, D_MODEL), F32),
        grid=(t // tm,),
        in_specs=[row] + [const(a) for a in (ln2, wup, wdn, fnorm)],
        out_specs=row,
        compiler_params=pltpu.CompilerParams(dimension_semantics=("arbitrary",), vmem_limit_bytes=VMEM_LIMIT),
        name="ffn",
    )(x2, ln2, wup, wdn, fnorm)


def _prep_layer(l, ln1, w_in, a_ln_g, a_ln_b, w_s, b_s, conv_w, a_log, dt_bias, o_norm, p_a, p_b, w_o, ln2,
                w_up, w_down):
    w = w_in[l]
    o_uv, o_qkv, o_z, o_b, o_a, o_g = 0, 2 * D_A, 2 * D_A + 3 * D_B, 2 * D_A + 4 * D_B, 2 * D_A + 4 * D_B + B_HEADS, \
        2 * D_A + 4 * D_B + 2 * B_HEADS
    wba = jnp.zeros((D_MODEL, BA_COLS), F32).at[:, :2 * B_HEADS].set(w[:, o_b:o_g])
    pad_a = lambda v: jnp.zeros((1, BA_COLS), F32).at[0, A_COL0:A_COL0 + B_HEADS].set(v)
    row = lambda v: v.reshape(1, -1)
    return dict(
        ln1=row(ln1[l]), wuv=w[:, o_uv:o_qkv].astype(BF16), wqkv=w[:, o_qkv:o_z].astype(BF16),
        wz=w[:, o_z:o_b].astype(BF16), wba=wba.astype(BF16), wg=w[:, o_g:].astype(BF16),
        alg=row(a_ln_g[l]), alb=row(a_ln_b[l]), ws=w_s[l], bst=b_s[l].T, cw=conv_w[l].T,
        alog=pad_a(a_log[l]), dtb=pad_a(dt_bias[l]), onorm=row(o_norm[l]),
        pa=p_a[l].astype(BF16), pb=p_b[l].astype(BF16), wo=w_o[l].astype(BF16),
        ln2=row(ln2[l]), wup=w_up[l].astype(BF16), wdn=w_down[l].astype(BF16))


def _layer(x, conv0, s0, p, fnorm, *, valid, final):
    b, l, _ = x.shape
    t = b * l
    outs = _inproj(x.reshape(t, D_MODEL), p["ln1"], p["wuv"], p["wqkv"], p["wz"], p["wba"], p["wg"], p["alg"], p["alb"])
    ug, vn, qkv, sz, ba, sga, sgb = [o.reshape(b, l, -1) for o in outs]
    y, conv_state, s_new = _mixer(ug, vn, qkv, sz, ba, sga, sgb, x, conv0, s0, p["ws"], p["bst"], p["cw"], p["alog"],
                                  p["dtb"], p["onorm"], p["pa"], p["pb"], p["wo"], valid=valid)
    out = _ffn(y.reshape(t, D_MODEL), p["ln2"], p["wup"], p["wdn"], fnorm, final=final)
    return out.reshape(b, l, D_MODEL), conv_state, s_new, vn


def kernel(x_prompt, x_sample, state_conv, state_delta, ln1, w_in, a_ln_g, a_ln_b, w_s, b_s, conv_w, a_log, dt_bias,
           o_norm, p_a, p_b, w_o, ln2, w_up, w_down, final_norm):
    depth = w_in.shape[0]
    bp, lp, _ = x_prompt.shape
    bs, ls, _ = x_sample.shape
    ls_pad = -(-ls // BLK) * BLK
    yp = x_prompt
    ys = jnp.pad(x_sample, ((0, 0), (0, ls_pad - ls), (0, 0)))
    zero_conv = jnp.zeros((bp, CONV_W - 1, 3 * D_B), F32)
    zero_delta = jnp.zeros((bp, B_HEADS, B_DK, B_DV), F32)
    fnorm = final_norm.reshape(1, -1)
    conv_p, delta_p, conv_s, delta_s, gv_s = [], [], [], [], []
    for l in range(depth):
        p = _prep_layer(l, ln1, w_in, a_ln_g, a_ln_b, w_s, b_s, conv_w, a_log, dt_bias, o_norm, p_a, p_b, w_o, ln2,
                        w_up, w_down)
        final = l == depth - 1
        yp, cp, dp, _ = _layer(yp, zero_conv, zero_delta, p, fnorm, valid=min(TS_MIXER, lp), final=final)
        ys, cs, ds, vs = _layer(ys, state_conv[l], state_delta[l], p, fnorm, valid=ls, final=final)
        conv_p.append(cp); delta_p.append(dp); conv_s.append(cs); delta_s.append(ds); gv_s.append(vs[:, :ls])
    return (yp, ys[:, :ls], jnp.stack(conv_p), jnp.stack(delta_p), jnp.stack(conv_s), jnp.stack(delta_s),
            jnp.stack(gv_s))
```

```python
import functools

import jax
import jax.numpy as jnp
from jax import lax
from jax.experimental import pallas as pl
from jax.experimental.pallas import tpu as pltpu

F32 = jnp.float32
BF16 = jnp.bfloat16

D_MODEL = 1024
D_A = 1024
A_GROUPS = 8
A_GROUP_W = D_A // A_GROUPS
B_HEADS = 8
B_DK = 128
B_DV = 128
D_B = B_HEADS * B_DV
CONV_W = 4
D_FF = 4 * D_MODEL
EPS = 1e-6
GMLP_CHUNK = 128

LANES = 128
SUBLANES = 8
BLK = 128
BA_COLS = LANES
A_COL0 = B_HEADS
NEG_BIG = -1e30
VMEM_LIMIT = 60000 * 1024

TM_INPROJ = 256
TM_FFN = 512
FFN_CHUNK = 1024
TS_MIXER = 256


def _rms(x, g):
    return x * lax.rsqrt(jnp.mean(x * x, axis=-1, keepdims=True) + EPS) * g


def _gelu(x):
    return 0.5 * x * (1.0 + lax.erf(x * 0.7071067811865476))


def _silu(x):
    return x * jax.nn.sigmoid(x)


def _dot(a, b):
    return jnp.dot(a, b, preferred_element_type=F32)


def _dot_nt(a, b):
    return lax.dot_general(a, b, (((1,), (1,)), ((), ())), preferred_element_type=F32)


def _split3(v):
    hi = v.astype(BF16)
    r1 = v - hi.astype(F32)
    mid = r1.astype(BF16)
    lo = (r1 - mid.astype(F32)).astype(BF16)
    return hi, mid, lo


def _inproj_kernel(x_ref, ln1_ref, wuv_ref, wqkv_ref, wz_ref, wba_ref, wg_ref, alg_ref, alb_ref,
                   ug_ref, vn_ref, qkv_ref, sz_ref, ba_ref, sga_ref, sgb_ref):
    h = _rms(x_ref[...], ln1_ref[...]).astype(BF16)
    uv = _dot(h, wuv_ref[...])
    ug_ref[...] = _gelu(uv[:, :D_A])
    va = _gelu(uv[:, D_A:])
    mu = jnp.mean(va, axis=-1, keepdims=True)
    vc = va - mu
    var = jnp.mean(vc * vc, axis=-1, keepdims=True)
    vn_ref[...] = vc * lax.rsqrt(var + EPS) * alg_ref[...] + alb_ref[...]
    qkv_ref[...] = _dot(h, wqkv_ref[...])
    sz_ref[...] = _silu(_dot(h, wz_ref[...]))
    ba_ref[...] = _dot(h, wba_ref[...])
    gates = jax.nn.sigmoid(_dot(h, wg_ref[...]))
    sga_ref[...] = gates[:, :D_MODEL]
    sgb_ref[...] = gates[:, D_MODEL:]


def _inproj(x2, ln1, wuv, wqkv, wz, wba, wg, alg, alb):
    t = x2.shape[0]
    tm = min(TM_INPROJ, t)
    row = lambda c: pl.BlockSpec((tm, c), lambda i: (i, 0))
    const = lambda a: pl.BlockSpec(a.shape, lambda i: (0, 0), pipeline_mode=pl.Buffered(1))
    widths = (D_A, D_A, 3 * D_B, D_B, BA_COLS, D_MODEL, D_MODEL)
    return pl.pallas_call(
        _inproj_kernel,
        out_shape=[jax.ShapeDtypeStruct((t, c), F32) for c in widths],
        grid=(t // tm,),
        in_specs=[row(D_MODEL)] + [const(a) for a in (ln1, wuv, wqkv, wz, wba, wg, alg, alb)],
        out_specs=[row(c) for c in widths],
        compiler_params=pltpu.CompilerParams(dimension_semantics=("arbitrary",), vmem_limit_bytes=VMEM_LIMIT),
        name="inproj",
    )(x2, ln1, wuv, wqkv, wz, wba, wg, alg, alb)


def _level_masks():
    row = lax.broadcasted_iota(jnp.int32, (BLK, BLK), 0)
    col = lax.broadcasted_iota(jnp.int32, (BLK, BLK), 1)
    x = row ^ col
    masks = []
    b, k = 1, 0
    while b < BLK:
        masks.append(jnp.where(((x >> k) == 1) & ((row & b) != 0), 1.0, 0.0).astype(F32))
        b, k = 2 * b, k + 1
    return row, col, masks


def _mixer_kernel(ug_ref, vn_ref, qkv_ref, sz_ref, ba_ref, sga_ref, sgb_ref, x_ref,
                  conv0_ref, s0_ref, ws_ref, bst_ref, cw_ref, alog_ref, dtb_ref, on_ref,
                  pa_ref, pb_ref, wo_ref,
                  y_ref, convo_ref, so_ref,
                  xpad_ref, s_ref, za_ref, zb_ref, *, ts, valid):
    step = pl.program_id(1)
    nblk = ts // BLK

    @pl.when(step == 0)
    def _():
        xpad_ref[0:SUBLANES, :] = jnp.zeros((SUBLANES, 3 * D_B), F32)
        xpad_ref[SUBLANES - (CONV_W - 1):SUBLANES, :] = conv0_ref[...]
        s_ref[...] = s0_ref[...]

    xpad_ref[SUBLANES:SUBLANES + ts, :] = qkv_ref[...]
    convo_ref[...] = xpad_ref[SUBLANES + valid - (CONV_W - 1):SUBLANES + valid, :]

    def conv_slab(t0, c0):
        acc = None
        for j in range(CONV_W):
            r0 = t0 + SUBLANES - (CONV_W - 1) + j
            term = xpad_ref[r0:r0 + BLK, c0:c0 + LANES] * cw_ref[j:j + 1, c0:c0 + LANES]
            acc = term if acc is None else acc + term
        return _silu(acc)

    row, col, masks = _level_masks()
    causal = row >= col
    strict = row > col
    eye = jnp.where(row == col, 1.0, 0.0).astype(F32)
    tri01 = jnp.where(causal, 1.0, 0.0).astype(BF16)

    gchunk = min(GMLP_CHUNK, ts)
    for c in range(ts // gchunk):
        r0 = c * gchunk
        for g in range(A_GROUPS):
            c0 = g * A_GROUP_W
            w = jnp.where(causal, ws_ref[g], 0.0).astype(BF16)
            s = _dot(w, vn_ref[r0:r0 + gchunk, c0:c0 + A_GROUP_W].astype(BF16)) + bst_ref[:, g:g + 1]
            za_ref[r0:r0 + gchunk, c0:c0 + A_GROUP_W] = (ug_ref[r0:r0 + gchunk, c0:c0 + A_GROUP_W] * s).astype(BF16)

    pairs = [(c, h) for c in range(nblk) for h in range(B_HEADS)]
    per_blk = []
    for c in range(nblk):
        r0 = c * BLK
        ba = ba_ref[r0:r0 + BLK, :]
        beta_c = jax.nn.sigmoid(ba)
        g_c = -jnp.exp(alog_ref[...]) * jax.nn.softplus(ba + dtb_ref[...])
        if valid < ts:
            live = lax.broadcasted_iota(jnp.int32, (BLK, BA_COLS), 0) + r0 < valid
            beta_c = jnp.where(live, beta_c, 0.0)
            g_c = jnp.where(live, g_c, 0.0)
        gam_c = sum(_dot(tri01, p) for p in _split3(g_c))
        per_blk.append((beta_c, gam_c, gam_c.T))

    st = {}
    for c, h in pairs:
        r0, hc = c * BLK, h * B_DK
        beta_c, gam_c, gam_t = per_blk[c]
        q = conv_slab(r0, hc)
        k = conv_slab(r0, D_B + hc)
        v = conv_slab(r0, 2 * D_B + hc)
        q = q * lax.rsqrt(jnp.sum(q * q, axis=-1, keepdims=True) + EPS) * (B_DK ** -0.5)
        k = k * lax.rsqrt(jnp.sum(k * k, axis=-1, keepdims=True) + EPS)
        beta = beta_c[:, h:h + 1]
        gam = jnp.broadcast_to(gam_c[:, A_COL0 + h:A_COL0 + h + 1], (BLK, LANES))
        gam_row = gam_t[A_COL0 + h:A_COL0 + h + 1, :]
        gam_last = gam[BLK - 1:BLK, :]
        eg = jnp.exp(gam)
        kb = k * beta
        st[c, h] = dict(
            decay=jnp.exp(jnp.where(causal, gam - gam_row, NEG_BIG)),
            q16=q.astype(BF16), k16=k.astype(BF16), kb16=kb.astype(BF16),
            rhs=jnp.concatenate([v * beta, kb * eg], axis=1).astype(BF16),
            qg=(q * eg).astype(BF16), kgt=(k * jnp.exp(gam_last - gam)).T.astype(BF16),
            gl=jnp.exp(gam_last))
    for pr in pairs:
        d = st[pr]
        d["kk"] = _dot_nt(d["kb16"], d["k16"])
        d["qk"] = _dot_nt(d["q16"], d["k16"])
    for pr in pairs:
        d = st[pr]
        d["a"] = jnp.where(strict, d["kk"] * d["decay"], 0.0)
        d["p"] = (d["qk"] * d["decay"]).astype(BF16)
        d["x"] = eye - d["a"] * masks[0]
    for m in masks[1:]:
        for pr in pairs:
            d = st[pr]
            d["xb"] = d["x"].astype(BF16)
            d["y"] = _dot(d["xb"], (d["a"] * m).astype(BF16)).astype(BF16)
        for pr in pairs:
            d = st[pr]
            d["x"] = d["x"] - _dot(d["y"], d["xb"])
    for pr in pairs:
        d = st[pr]
        uw = _dot(d["x"].astype(BF16), d["rhs"])
        d["u"] = uw[:, :B_DV]
        d["w"] = uw[:, B_DV:].astype(BF16)

    for c in range(nblk):
        r0 = c * BLK
        cur = {}
        for h in range(B_HEADS):
            s_old = s_ref[h]
            s16 = s_old.astype(BF16)
            cur[h] = (s_old, _dot(st[c, h]["w"], s16), _dot(st[c, h]["qg"], s16))
        for h in range(B_HEADS):
            d = st[c, h]
            s_old, ws, qs = cur[h]
            v16 = (d["u"] - ws).astype(BF16)
            cur[h] = (s_old, qs + _dot(d["p"], v16), _dot(d["kgt"], v16))
        for h in range(B_HEADS):
            d = st[c, h]
            s_old, o, ds = cur[h]
            s_ref[h] = s_old * d["gl"] + ds
            o = o * lax.rsqrt(jnp.mean(o * o, axis=-1, keepdims=True) + EPS) * on_ref[...]
            hc = h * B_DV
            zb_ref[r0:r0 + BLK, hc:hc + B_DV] = (o * sz_ref[r0:r0 + BLK, hc:hc + B_DV]).astype(BF16)

    xpad_ref[0:SUBLANES, :] = xpad_ref[ts:ts + SUBLANES, :]

    @pl.when(step == pl.num_programs(1) - 1)
    def _():
        so_ref[...] = s_ref[...]

    mix = sga_ref[...] * _dot(za_ref[...], pa_ref[...]) + sgb_ref[...] * _dot(zb_ref[...], pb_ref[...])
    y_ref[...] = x_ref[...] + _dot(mix.astype(BF16), wo_ref[...])


def _mixer(ug, vn, qkv, sz, ba, sga, sgb, x, conv0, s0, ws, bst, cw, alog, dtb, onorm, pa, pb, wo, *, valid):
    b, l, _ = x.shape
    ts = min(TS_MIXER, l)
    assert l % ts == 0 and ts % BLK == 0
    assert valid == ts or l == ts
    tok = lambda c: pl.BlockSpec((None, ts, c), lambda i, j: (i, j, 0))
    const = lambda a: pl.BlockSpec(a.shape, lambda i, j: (0,) * a.ndim, pipeline_mode=pl.Buffered(1))
    per_seq = lambda a: pl.BlockSpec((None,) + a.shape[1:], lambda i, j: (i,) + (0,) * (a.ndim - 1))
    consts = (ws, bst, cw, alog, dtb, onorm, pa, pb, wo)
    return pl.pallas_call(
        functools.partial(_mixer_kernel, ts=ts, valid=valid),
        out_shape=[jax.ShapeDtypeStruct((b, l, D_MODEL), F32),
                   jax.ShapeDtypeStruct((b, CONV_W - 1, 3 * D_B), F32),
                   jax.ShapeDtypeStruct((b, B_HEADS, B_DK, B_DV), F32)],
        grid=(b, l // ts),
        in_specs=[tok(D_A), tok(D_A), tok(3 * D_B), tok(D_B), tok(BA_COLS), tok(D_MODEL), tok(D_MODEL), tok(D_MODEL),
                  per_seq(conv0), per_seq(s0)] + [const(a) for a in consts],
        out_specs=[tok(D_MODEL),
                   pl.BlockSpec((None, CONV_W - 1, 3 * D_B), lambda i, j: (i, 0, 0)),
                   pl.BlockSpec((None, B_HEADS, B_DK, B_DV), lambda i, j: (i, 0, 0, 0))],
        scratch_shapes=[pltpu.VMEM((ts + SUBLANES, 3 * D_B), F32),
                        pltpu.VMEM((B_HEADS, B_DK, B_DV), F32),
                        pltpu.VMEM((ts, D_A), BF16),
                        pltpu.VMEM((ts, D_B), BF16)],
        compiler_params=pltpu.CompilerParams(dimension_semantics=("arbitrary", "arbitrary"),
                                             vmem_limit_bytes=VMEM_LIMIT),
        name="mixer",
    )(ug, vn, qkv, sz, ba, sga, sgb, x, conv0, s0, *consts)


def _ffn_kernel(x_ref, ln2_ref, wup_ref, wdn_ref, fn_ref, o_ref, *, final):
    x = x_ref[...]
    h = _rms(x, ln2_ref[...]).astype(BF16)
    acc = x
    for c in range(D_FF // FFN_CHUNK):
        c0 = c * FFN_CHUNK
        hid = _dot(h, wup_ref[:, c0:c0 + FFN_CHUNK])
        act = jnp.square(jnp.maximum(hid, 0.0)).astype(BF16)
        acc = acc + _dot(act, wdn_ref[c0:c0 + FFN_CHUNK, :])
    o_ref[...] = _rms(acc, fn_ref[...]) if final else acc


def _ffn(x2, ln2, wup, wdn, fnorm, *, final):
    t = x2.shape[0]
    tm = min(TM_FFN, t)
    row = pl.BlockSpec((tm, D_MODEL), lambda i: (i, 0))
    const = lambda a: pl.BlockSpec(a.shape, lambda i: (0, 0), pipeline_mode=pl.Buffered(1))
    return pl.pallas_call(
        functools.partial(_ffn_kernel, final=final),
        out_shape=jax.ShapeDtypeStruct((t, D_MODEL), F32),
        grid=(t // tm,),
        in_specs=[row] + [const(a) for a in (ln2, wup, wdn, fnorm)],
        out_specs=row,
        compiler_params=pltpu.CompilerParams(dimension_semantics=("arbitrary",), vmem_limit_bytes=VMEM_LIMIT),
        name="ffn",
    )(x2, ln2, wup, wdn, fnorm)


def _prep_layer(l, ln1, w_in, a_ln_g, a_ln_b, w_s, b_s, conv_w, a_log, dt_bias, o_norm, p_a, p_b, w_o, ln2,
                w_up, w_down):
    w = w_in[l]
    o_uv, o_qkv, o_z, o_b, o_a, o_g = 0, 2 * D_A, 2 * D_A + 3 * D_B, 2 * D_A + 4 * D_B, 2 * D_A + 4 * D_B + B_HEADS, \
        2 * D_A + 4 * D_B + 2 * B_HEADS
    wba = jnp.zeros((D_MODEL, BA_COLS), F32).at[:, :2 * B_HEADS].set(w[:, o_b:o_g])
    pad_a = lambda v: jnp.zeros((1, BA_COLS), F32).at[0, A_COL0:A_COL0 + B_HEADS].set(v)
    row = lambda v: v.reshape(1, -1)
    return dict(
        ln1=row(ln1[l]), wuv=w[:, o_uv:o_qkv].astype(BF16), wqkv=w[:, o_qkv:o_z].astype(BF16),
        wz=w[:, o_z:o_b].astype(BF16), wba=wba.astype(BF16), wg=w[:, o_g:].astype(BF16),
        alg=row(a_ln_g[l]), alb=row(a_ln_b[l]), ws=w_s[l], bst=b_s[l].T, cw=conv_w[l].T,
        alog=pad_a(a_log[l]), dtb=pad_a(dt_bias[l]), onorm=row(o_norm[l]),
        pa=p_a[l].astype(BF16), pb=p_b[l].astype(BF16), wo=w_o[l].astype(BF16),
        ln2=row(ln2[l]), wup=w_up[l].astype(BF16), wdn=w_down[l].astype(BF16))


def _layer(x, conv0, s0, p, fnorm, *, valid, final):
    b, l, _ = x.shape
    t = b * l
    outs = _inproj(x.reshape(t, D_MODEL), p["ln1"], p["wuv"], p["wqkv"], p["wz"], p["wba"], p["wg"], p["alg"], p["alb"])
    ug, vn, qkv, sz, ba, sga, sgb = [o.reshape(b, l, -1) for o in outs]
    y, conv_state, s_new = _mixer(ug, vn, qkv, sz, ba, sga, sgb, x, conv0, s0, p["ws"], p["bst"], p["cw"], p["alog"],
                                  p["dtb"], p["onorm"], p["pa"], p["pb"], p["wo"], valid=valid)
    out = _ffn(y.reshape(t, D_MODEL), p["ln2"], p["wup"], p["wdn"], fnorm, final=final)
    return out.reshape(b, l, D_MODEL), conv_state, s_new, vn


def kernel(x_prompt, x_sample, state_conv, state_delta, ln1, w_in, a_ln_g, a_ln_b, w_s, b_s, conv_w, a_log, dt_bias,
           o_norm, p_a, p_b, w_o, ln2, w_up, w_down, final_norm):
    depth = w_in.shape[0]
    bp, lp, _ = x_prompt.shape
    bs, ls, _ = x_sample.shape
    ls_pad = -(-ls // BLK) * BLK
    yp = x_prompt
    ys = jnp.pad(x_sample, ((0, 0), (0, ls_pad - ls), (0, 0)))
    zero_conv = jnp.zeros((bp, CONV_W - 1, 3 * D_B), F32)
    zero_delta = jnp.zeros((bp, B_HEADS, B_DK, B_DV), F32)
    fnorm = final_norm.reshape(1, -1)
    conv_p, delta_p, conv_s, delta_s, gv_s = [], [], [], [], []
    for l in range(depth):
        p = _prep_layer(l, ln1, w_in, a_ln_g, a_ln_b, w_s, b_s, conv_w, a_log, dt_bias, o_norm, p_a, p_b, w_o, ln2,
                        w_up, w_down)
        final = l == depth - 1
        yp, cp, dp, _ = _layer(yp, zero_conv, zero_delta, p, fnorm, valid=min(TS_MIXER, lp), final=final)
        ys, cs, ds, vs = _layer(ys, state_conv[l], state_delta[l], p, fnorm, valid=ls, final=final)
        conv_p.append(cp); delta_p.append(dp); conv_s.append(cs); delta_s.append(ds); gv_s.append(vs[:, :ls])
    return (yp, ys[:, :ls], jnp.stack(conv_p), jnp.stack(delta_p), jnp.stack(conv_s), jnp.stack(delta_s),
            jnp.stack(gv_s))
```

```python
import functools

import jax
import jax.numpy as jnp
from jax import lax
from jax.experimental import pallas as pl
from jax.experimental.pallas import tpu as pltpu

F32 = jnp.float32
BF16 = jnp.bfloat16

D_MODEL = 1024
D_A = 1024
A_GROUPS = 8
A_GROUP_W = D_A // A_GROUPS
B_HEADS = 8
B_DK = 128
B_DV = 128
D_B = B_HEADS * B_DV
CONV_W = 4
D_FF = 4 * D_MODEL
EPS = 1e-6
GMLP_CHUNK = 128

LANES = 128
SUBLANES = 8
BLK = 128
BA_COLS = LANES
A_COL0 = B_HEADS
NEG_BIG = -1e30
VMEM_LIMIT = 60000 * 1024

TM_FFN = 512
FFN_CHUNK = 1024
TS_MIXER = 256
PROJ_CHUNK = 1024


def _rms(x, g):
    return x * lax.rsqrt(jnp.mean(x * x, axis=-1, keepdims=True) + EPS) * g


def _gelu(x):
    return 0.5 * x * (1.0 + lax.erf(x * 0.7071067811865476))


def _silu(x):
    return x * jax.nn.sigmoid(x)


def _dot(a, b):
    return jnp.dot(a, b, preferred_element_type=F32)


def _dot_nt(a, b):
    return lax.dot_general(a, b, (((1,), (1,)), ((), ())), preferred_element_type=F32)


def _split3(v):
    hi = v.astype(BF16)
    r1 = v - hi.astype(F32)
    mid = r1.astype(BF16)
    lo = (r1 - mid.astype(F32)).astype(BF16)
    return hi, mid, lo


def _level_masks():
    row = lax.broadcasted_iota(jnp.int32, (BLK, BLK), 0)
    col = lax.broadcasted_iota(jnp.int32, (BLK, BLK), 1)
    x = row ^ col
    masks = []
    b, k = 1, 0
    while b < BLK:
        masks.append(jnp.where(((x >> k) == 1) & ((row & b) != 0), 1.0, 0.0).astype(F32))
        b, k = 2 * b, k + 1
    return row, col, masks


def _mixer_kernel(x_ref, conv0_ref, s0_ref,
                  ln1_ref, wuv_ref, wqkv_ref, wz_ref, wba_ref, wg_ref, alg_ref, alb_ref,
                  ws_ref, bst_ref, cw_ref, alog_ref, dtb_ref, on_ref, pa_ref, pb_ref, wo_ref,
                  y_ref, convo_ref, so_ref, vn_ref,
                  xpad_ref, s_ref, ug_ref, sz_ref, ba_ref, sga_ref, sgb_ref, za_ref, zb_ref, *, ts, valid):
    step = pl.program_id(1)
    nblk = ts // BLK

    @pl.when(step == 0)
    def _():
        xpad_ref[0:SUBLANES, :] = jnp.zeros((SUBLANES, 3 * D_B), F32)
        xpad_ref[SUBLANES - (CONV_W - 1):SUBLANES, :] = conv0_ref[...]
        s_ref[...] = s0_ref[...]

    h = _rms(x_ref[...], ln1_ref[...]).astype(BF16)
    for c0 in range(0, 3 * D_B, PROJ_CHUNK):
        xpad_ref[SUBLANES:SUBLANES + ts, c0:c0 + PROJ_CHUNK] = _dot(h, wqkv_ref[:, c0:c0 + PROJ_CHUNK])
    ba_ref[...] = _dot(h, wba_ref[...])
    ug_ref[...] = _gelu(_dot(h, wuv_ref[:, :D_A]))
    va = _gelu(_dot(h, wuv_ref[:, D_A:]))
    mu = jnp.mean(va, axis=-1, keepdims=True)
    vc = va - mu
    var = jnp.mean(vc * vc, axis=-1, keepdims=True)
    vn_ref[...] = vc * lax.rsqrt(var + EPS) * alg_ref[...] + alb_ref[...]
    sz_ref[...] = _silu(_dot(h, wz_ref[...]))
    sga_ref[...] = jax.nn.sigmoid(_dot(h, wg_ref[:, :D_MODEL]))
    sgb_ref[...] = jax.nn.sigmoid(_dot(h, wg_ref[:, D_MODEL:]))

    convo_ref[...] = xpad_ref[SUBLANES + valid - (CONV_W - 1):SUBLANES + valid, :]

    def conv_slab(t0, c0):
        acc = None
        for j in range(CONV_W):
            r0 = t0 + SUBLANES - (CONV_W - 1) + j
            term = xpad_ref[r0:r0 + BLK, c0:c0 + LANES] * cw_ref[j:j + 1, c0:c0 + LANES]
            acc = term if acc is None else acc + term
        return _silu(acc)

    row, col, masks = _level_masks()
    causal = row >= col
    strict = row > col
    eye = jnp.where(row == col, 1.0, 0.0).astype(F32)
    tri01 = jnp.where(causal, 1.0, 0.0).astype(BF16)

    gchunk = min(GMLP_CHUNK, ts)
    for c in range(ts // gchunk):
        r0 = c * gchunk
        for g in range(A_GROUPS):
            c0 = g * A_GROUP_W
            w = jnp.where(causal, ws_ref[g], 0.0).astype(BF16)
            s = _dot(w, vn_ref[r0:r0 + gchunk, c0:c0 + A_GROUP_W].astype(BF16)) + bst_ref[:, g:g + 1]
            za_ref[r0:r0 + gchunk, c0:c0 + A_GROUP_W] = (ug_ref[r0:r0 + gchunk, c0:c0 + A_GROUP_W] * s).astype(BF16)

    pairs = [(c, h) for c in range(nblk) for h in range(B_HEADS)]
    per_blk = []
    for c in range(nblk):
        r0 = c * BLK
        ba = ba_ref[r0:r0 + BLK, :]
        beta_c = jax.nn.sigmoid(ba)
        g_c = -jnp.exp(alog_ref[...]) * jax.nn.softplus(ba + dtb_ref[...])
        if valid < ts:
            live = lax.broadcasted_iota(jnp.int32, (BLK, BA_COLS), 0) + r0 < valid
            beta_c = jnp.where(live, beta_c, 0.0)
            g_c = jnp.where(live, g_c, 0.0)
        gam_c = sum(_dot(tri01, p) for p in _split3(g_c))
        per_blk.append((beta_c, gam_c, gam_c.T))

    st = {}
    for c, h in pairs:
        r0, hc = c * BLK, h * B_DK
        beta_c, gam_c, gam_t = per_blk[c]
        q = conv_slab(r0, hc)
        k = conv_slab(r0, D_B + hc)
        v = conv_slab(r0, 2 * D_B + hc)
        q = q * lax.rsqrt(jnp.sum(q * q, axis=-1, keepdims=True) + EPS) * (B_DK ** -0.5)
        k = k * lax.rsqrt(jnp.sum(k * k, axis=-1, keepdims=True) + EPS)
        beta = beta_c[:, h:h + 1]
        gam = jnp.broadcast_to(gam_c[:, A_COL0 + h:A_COL0 + h + 1], (BLK, LANES))
        gam_row = gam_t[A_COL0 + h:A_COL0 + h + 1, :]
        gam_last = gam[BLK - 1:BLK, :]
        eg = jnp.exp(gam)
        kb = k * beta
        st[c, h] = dict(
            decay=jnp.exp(jnp.where(causal, gam - gam_row, NEG_BIG)),
            q16=q.astype(BF16), k16=k.astype(BF16), kb16=kb.astype(BF16),
            rhs=jnp.concatenate([v * beta, kb * eg], axis=1).astype(BF16),
            qg=(q * eg).astype(BF16), kgt=(k * jnp.exp(gam_last - gam)).T.astype(BF16),
            gl=jnp.exp(gam_last))
    for pr in pairs:
        d = st[pr]
        d["kk"] = _dot_nt(d["kb16"], d["k16"])
        d["qk"] = _dot_nt(d["q16"], d["k16"])
    for pr in pairs:
        d = st[pr]
        d["a"] = jnp.where(strict, d["kk"] * d["decay"], 0.0)
        d["p"] = (d["qk"] * d["decay"]).astype(BF16)
        d["x"] = eye - d["a"] * masks[0]
    for m in masks[1:]:
        for pr in pairs:
            d = st[pr]
            d["xb"] = d["x"].astype(BF16)
            d["y"] = _dot(d["xb"], (d["a"] * m).astype(BF16)).astype(BF16)
        for pr in pairs:
            d = st[pr]
            d["x"] = d["x"] - _dot(d["y"], d["xb"])
    for pr in pairs:
        d = st[pr]
        uw = _dot(d["x"].astype(BF16), d["rhs"])
        d["u"] = uw[:, :B_DV]
        d["w"] = uw[:, B_DV:].astype(BF16)

    for c in range(nblk):
        r0 = c * BLK
        cur = {}
        for h in range(B_HEADS):
            s_old = s_ref[h]
            s16 = s_old.astype(BF16)
            cur[h] = (s_old, _dot(st[c, h]["w"], s16), _dot(st[c, h]["qg"], s16))
        for h in range(B_HEADS):
            d = st[c, h]
            s_old, ws, qs = cur[h]
            v16 = (d["u"] - ws).astype(BF16)
            cur[h] = (s_old, qs + _dot(d["p"], v16), _dot(d["kgt"], v16))
        for h in range(B_HEADS):
            d = st[c, h]
            s_old, o, ds = cur[h]
            s_ref[h] = s_old * d["gl"] + ds
            o = o * lax.rsqrt(jnp.mean(o * o, axis=-1, keepdims=True) + EPS) * on_ref[...]
            hc = h * B_DV
            zb_ref[r0:r0 + BLK, hc:hc + B_DV] = (o * sz_ref[r0:r0 + BLK, hc:hc + B_DV]).astype(BF16)

    xpad_ref[0:SUBLANES, :] = xpad_ref[ts:ts + SUBLANES, :]

    @pl.when(step == pl.num_programs(1) - 1)
    def _():
        so_ref[...] = s_ref[...]

    mix = sga_ref[...] * _dot(za_ref[...], pa_ref[...]) + sgb_ref[...] * _dot(zb_ref[...], pb_ref[...])
    y_ref[...] = x_ref[...] + _dot(mix.astype(BF16), wo_ref[...])


def _mixer(x, conv0, s0, consts, *, valid):
    b, l, _ = x.shape
    ts = min(TS_MIXER, l)
    assert l % ts == 0 and ts % BLK == 0
    assert valid == ts or l == ts
    tok = lambda c: pl.BlockSpec((None, ts, c), lambda i, j: (i, j, 0))
    const = lambda a: pl.BlockSpec(a.shape, lambda i, j: (0,) * a.ndim, pipeline_mode=pl.Buffered(1))
    per_seq = lambda a: pl.BlockSpec((None,) + a.shape[1:], lambda i, j: (i,) + (0,) * (a.ndim - 1))
    tile = lambda c, dt: pltpu.VMEM((ts, c), dt)
    return pl.pallas_call(
        functools.partial(_mixer_kernel, ts=ts, valid=valid),
        out_shape=[jax.ShapeDtypeStruct((b, l, D_MODEL), F32),
                   jax.ShapeDtypeStruct((b, CONV_W - 1, 3 * D_B), F32),
                   jax.ShapeDtypeStruct((b, B_HEADS, B_DK, B_DV), F32),
                   jax.ShapeDtypeStruct((b, l, D_A), F32)],
        grid=(b, l // ts),
        in_specs=[tok(D_MODEL), per_seq(conv0), per_seq(s0)] + [const(a) for a in consts],
        out_specs=[tok(D_MODEL),
                   pl.BlockSpec((None, CONV_W - 1, 3 * D_B), lambda i, j: (i, 0, 0)),
                   pl.BlockSpec((None, B_HEADS, B_DK, B_DV), lambda i, j: (i, 0, 0, 0)),
                   tok(D_A)],
        scratch_shapes=[pltpu.VMEM((ts + SUBLANES, 3 * D_B), F32),
                        pltpu.VMEM((B_HEADS, B_DK, B_DV), F32),
                        tile(D_A, F32), tile(D_B, F32), tile(BA_COLS, F32), tile(D_MODEL, F32), tile(D_MODEL, F32),
                        tile(D_A, BF16), tile(D_B, BF16)],
        compiler_params=pltpu.CompilerParams(dimension_semantics=("arbitrary", "arbitrary"),
                                             vmem_limit_bytes=VMEM_LIMIT),
        name="mixer",
    )(x, conv0, s0, *consts)


def _ffn_kernel(x_ref, ln2_ref, wup_ref, wdn_ref, fn_ref, o_ref, *, final):
    x = x_ref[...]
    h = _rms(x, ln2_ref[...]).astype(BF16)
    acc = x
    for c in range(D_FF // FFN_CHUNK):
        c0 = c * FFN_CHUNK
        hid = _dot(h, wup_ref[:, c0:c0 + FFN_CHUNK])
        act = jnp.square(jnp.maximum(hid, 0.0)).astype(BF16)
        acc = acc + _dot(act, wdn_ref[c0:c0 + FFN_CHUNK, :])
    o_ref[...] = _rms(acc, fn_ref[...]) if final else acc


def _ffn(x2, ln2, wup, wdn, fnorm, *, final):
    t = x2.shape[0]
    tm = min(TM_FFN, t)
    row = pl.BlockSpec((tm, D_MODEL), lambda i: (i, 0))
    const = lambda a: pl.BlockSpec(a.shape, lambda i: (0, 0), pipeline_mode=pl.Buffered(1))
    return pl.pallas_call(
        functools.partial(_ffn_kernel, final=final),
        out_shape=jax.ShapeDtypeStruct((t, D_MODEL), F32),
        grid=(t // tm,),
        in_specs=[row] + [const(a) for a in (ln2, wup, wdn, fnorm)],
        out_specs=row,
        compiler_params=pltpu.CompilerParams(dimension_semantics=("arbitrary",), vmem_limit_bytes=VMEM_LIMIT),
        name="ffn",
    )(x2, ln2, wup, wdn, fnorm)


def _prep_layer(l, ln1, w_in, a_ln_g, a_ln_b, w_s, b_s, conv_w, a_log, dt_bias, o_norm, p_a, p_b, w_o, ln2,
                w_up, w_down):
    w = w_in[l]
    o_qkv = 2 * D_A
    o_z = o_qkv + 3 * D_B
    o_b = o_z + D_B
    o_g = o_b + 2 * B_HEADS
    wba = jnp.zeros((D_MODEL, BA_COLS), F32).at[:, :2 * B_HEADS].set(w[:, o_b:o_g])
    pad_a = lambda v: jnp.zeros((1, BA_COLS), F32).at[0, A_COL0:A_COL0 + B_HEADS].set(v)
    row = lambda v: v.reshape(1, -1)
    mixer_consts = (
        row(ln1[l]), w[:, :o_qkv].astype(BF16), w[:, o_qkv:o_z].astype(BF16), w[:, o_z:o_b].astype(BF16),
        wba.astype(BF16), w[:, o_g:].astype(BF16), row(a_ln_g[l]), row(a_ln_b[l]),
        w_s[l], b_s[l].T, conv_w[l].T, pad_a(a_log[l]), pad_a(dt_bias[l]), row(o_norm[l]),
        p_a[l].astype(BF16), p_b[l].astype(BF16), w_o[l].astype(BF16))
    ffn_consts = (row(ln2[l]), w_up[l].astype(BF16), w_down[l].astype(BF16))
    return mixer_consts, ffn_consts


def _layer(x, conv0, s0, consts, fnorm, *, valid, final):
    b, l, _ = x.shape
    mixer_consts, ffn_consts = consts
    y, conv_state, s_new, vn = _mixer(x, conv0, s0, mixer_consts, valid=valid)
    out = _ffn(y.reshape(b * l, D_MODEL), *ffn_consts, fnorm, final=final)
    return out.reshape(b, l, D_MODEL), conv_state, s_new, vn


def kernel(x_prompt, x_sample, state_conv, state_delta, ln1, w_in, a_ln_g, a_ln_b, w_s, b_s, conv_w, a_log, dt_bias,
           o_norm, p_a, p_b, w_o, ln2, w_up, w_down, final_norm):
    depth = w_in.shape[0]
    bp, lp, _ = x_prompt.shape
    bs, ls, _ = x_sample.shape
    ls_pad = -(-ls // BLK) * BLK
    yp = x_prompt
    ys = jnp.pad(x_sample, ((0, 0), (0, ls_pad - ls), (0, 0)))
    zero_conv = jnp.zeros((bp, CONV_W - 1, 3 * D_B), F32)
    zero_delta = jnp.zeros((bp, B_HEADS, B_DK, B_DV), F32)
    fnorm = final_norm.reshape(1, -1)
    conv_p, delta_p, conv_s, delta_s, gv_s = [], [], [], [], []
    for l in range(depth):
        consts = _prep_layer(l, ln1, w_in, a_ln_g, a_ln_b, w_s, b_s, conv_w, a_log, dt_bias, o_norm, p_a, p_b, w_o,
                             ln2, w_up, w_down)
        final = l == depth - 1
        yp, cp, dp, _ = _layer(yp, zero_conv, zero_delta, consts, fnorm, valid=min(TS_MIXER, lp), final=final)
        ys, cs, ds, vs = _layer(ys, state_conv[l], state_delta[l], consts, fnorm, valid=ls, final=final)
        conv_p.append(cp); delta_p.append(dp); conv_s.append(cs); delta_s.append(ds); gv_s.append(vs[:, :ls])
    return (yp, ys[:, :ls], jnp.stack(conv_p), jnp.stack(delta_p), jnp.stack(conv_s), jnp.stack(delta_s),
            jnp.stack(gv_s))
```

```python
import functools

import jax
import jax.numpy as jnp
from jax import lax
from jax.experimental import pallas as pl
from jax.experimental.pallas import tpu as pltpu

F32 = jnp.float32
BF16 = jnp.bfloat16

D_MODEL = 1024
D_A = 1024
A_GROUPS = 8
A_GROUP_W = D_A // A_GROUPS
B_HEADS = 8
B_DK = 128
B_DV = 128
D_B = B_HEADS * B_DV
CONV_W = 4
D_FF = 4 * D_MODEL
EPS = 1e-6
GMLP_CHUNK = 128

LANES = 128
SUBLANES = 8
BLK = 128
BA_COLS = LANES
A_COL0 = B_HEADS
NEG_BIG = -1e30
VMEM_LIMIT = 60000 * 1024

TM_FFN = 512
FFN_CHUNK = 1024
TS_MIXER = 256
HEAD_COLS = B_DK + B_DK + B_DV
PROJ_CHUNK = 2 * HEAD_COLS


def _rms(x, g):
    return x * lax.rsqrt(jnp.mean(x * x, axis=-1, keepdims=True) + EPS) * g


def _gelu(x):
    return 0.5 * x * (1.0 + lax.erf(x * 0.7071067811865476))


def _sigmoid(x):
    return jax.nn.sigmoid(x)


def _silu(x):
    return x * _sigmoid(x)


def _dot(a, b):
    return jnp.dot(a, b, preferred_element_type=F32)


def _dot_nt(a, b):
    return lax.dot_general(a, b, (((1,), (1,)), ((), ())), preferred_element_type=F32)


def _split3(v):
    hi = v.astype(BF16)
    r1 = v - hi.astype(F32)
    mid = r1.astype(BF16)
    lo = (r1 - mid.astype(F32)).astype(BF16)
    return hi, mid, lo


def _level_masks():
    row = lax.broadcasted_iota(jnp.int32, (BLK, BLK), 0)
    col = lax.broadcasted_iota(jnp.int32, (BLK, BLK), 1)
    x = row ^ col
    masks = []
    b, k = 1, 0
    while b < BLK:
        masks.append(jnp.where(((x >> k) == 1) & ((row & b) != 0), 1.0, 0.0).astype(F32))
        b, k = 2 * b, k + 1
    return row, col, masks


def _mixer_kernel(x_ref, conv0_ref, s0_ref,
                  ln1_ref, wuv_ref, wqkv_ref, wz_ref, wba_ref, wg_ref, alg_ref, alb_ref,
                  ws_ref, bst_ref, cw_ref, alog_ref, dtb_ref, on_ref, pa_ref, pb_ref, wo_ref,
                  y_ref, convo_ref, so_ref, *rest, ts, valid):
    vn_ref, s_ref, ug_ref, sz_ref, ba_ref, sga_ref, sgb_ref, za_ref, zb_ref, *xpad_refs = rest
    step = pl.program_id(1)
    nblk = ts // BLK
    tail = slice(SUBLANES - (CONV_W - 1), SUBLANES)

    @pl.when(step == 0)
    def _():
        for i, xp in enumerate(xpad_refs):
            xp[0:SUBLANES, :] = jnp.zeros((SUBLANES, PROJ_CHUNK), F32)
            xp[tail, :] = conv0_ref[:, i * PROJ_CHUNK:(i + 1) * PROJ_CHUNK]
        s_ref[...] = s0_ref[...]

    xn = _rms(x_ref[...], ln1_ref[...]).astype(BF16)
    ba_ref[...] = _dot(xn, wba_ref[...])
    for i, xp in enumerate(xpad_refs):
        xp[SUBLANES:SUBLANES + ts, :] = _dot(xn, wqkv_ref[:, i * PROJ_CHUNK:(i + 1) * PROJ_CHUNK])

    def proj_u():
        ug_ref[...] = _gelu(_dot(xn, wuv_ref[:, :D_A]))

    def proj_v():
        va = _gelu(_dot(xn, wuv_ref[:, D_A:]))
        mu = jnp.mean(va, axis=-1, keepdims=True)
        vc = va - mu
        var = jnp.mean(vc * vc, axis=-1, keepdims=True)
        vn_ref[...] = vc * lax.rsqrt(var + EPS) * alg_ref[...] + alb_ref[...]

    def proj_z():
        sz_ref[...] = _silu(_dot(xn, wz_ref[...]))

    def proj_ga():
        sga_ref[...] = _sigmoid(_dot(xn, wg_ref[:, :D_MODEL]))

    def proj_gb():
        sgb_ref[...] = _sigmoid(_dot(xn, wg_ref[:, D_MODEL:]))

    for i, xp in enumerate(xpad_refs):
        convo_ref[:, i * PROJ_CHUNK:(i + 1) * PROJ_CHUNK] = xp[SUBLANES + valid - (CONV_W - 1):SUBLANES + valid, :]

    def conv_slab(t0, c0):
        xp, cc = xpad_refs[c0 // PROJ_CHUNK], c0 % PROJ_CHUNK
        acc = None
        for j in range(CONV_W):
            r0 = t0 + SUBLANES - (CONV_W - 1) + j
            term = xp[r0:r0 + BLK, cc:cc + LANES] * cw_ref[j:j + 1, c0:c0 + LANES]
            acc = term if acc is None else acc + term
        return _silu(acc)

    row, col, masks = _level_masks()
    causal = row >= col
    strict = row > col
    eye = jnp.where(row == col, 1.0, 0.0).astype(F32)
    tri01 = jnp.where(causal, 1.0, 0.0).astype(BF16)

    def spatial_gating():
        gchunk = min(GMLP_CHUNK, ts)
        for c in range(ts // gchunk):
            r0 = c * gchunk
            for g in range(A_GROUPS):
                c0 = g * A_GROUP_W
                w = jnp.where(causal, ws_ref[g], 0.0).astype(BF16)
                s = _dot(w, vn_ref[r0:r0 + gchunk, c0:c0 + A_GROUP_W].astype(BF16)) + bst_ref[:, g:g + 1]
                za_ref[r0:r0 + gchunk, c0:c0 + A_GROUP_W] = (
                    ug_ref[r0:r0 + gchunk, c0:c0 + A_GROUP_W] * s).astype(BF16)

    pairs = [(c, h) for c in range(nblk) for h in range(B_HEADS)]
    per_blk = []
    for c in range(nblk):
        r0 = c * BLK
        ba = ba_ref[r0:r0 + BLK, :]
        beta_c = _sigmoid(ba)
        g_c = -jnp.exp(alog_ref[...]) * jax.nn.softplus(ba + dtb_ref[...])
        if valid < ts:
            live = lax.broadcasted_iota(jnp.int32, (BLK, BA_COLS), 0) + r0 < valid
            beta_c = jnp.where(live, beta_c, 0.0)
            g_c = jnp.where(live, g_c, 0.0)
        gam_c = sum(_dot(tri01, p) for p in _split3(g_c))
        per_blk.append((beta_c, gam_c, gam_c.T))

    def prepare(c, h):
        r0 = c * BLK
        beta_c, gam_c, gam_t = per_blk[c]
        q = conv_slab(r0, h * HEAD_COLS)
        k = conv_slab(r0, h * HEAD_COLS + B_DK)
        v = conv_slab(r0, h * HEAD_COLS + 2 * B_DK)
        q = q * lax.rsqrt(jnp.sum(q * q, axis=-1, keepdims=True) + EPS) * (B_DK ** -0.5)
        k = k * lax.rsqrt(jnp.sum(k * k, axis=-1, keepdims=True) + EPS)
        beta = beta_c[:, h:h + 1]
        gam = jnp.broadcast_to(gam_c[:, A_COL0 + h:A_COL0 + h + 1], (BLK, LANES))
        gam_row = gam_t[A_COL0 + h:A_COL0 + h + 1, :]
        gam_last = gam[BLK - 1:BLK, :]
        eg = jnp.exp(gam)
        kb = k * beta
        k16 = k.astype(BF16)
        return dict(
            decay=jnp.exp(jnp.where(causal, gam - gam_row, NEG_BIG)),
            kk=_dot_nt(kb.astype(BF16), k16), qk=_dot_nt(q.astype(BF16), k16),
            vb=(v * beta).astype(BF16), kbg=(kb * eg).astype(BF16),
            qg=(q * eg).astype(BF16), kgt=(k * jnp.exp(gam_last - gam)).T.astype(BF16),
            gl=jnp.exp(gam_last))

    dense_stages = ((proj_u,), (proj_v,), (spatial_gating, proj_z), (proj_ga, proj_gb))
    heads_per_group = B_HEADS // len(dense_stages)
    st = {}
    for gi, stages in enumerate(dense_stages):
        for stage in stages:
            stage()
        for c in range(nblk):
            for h in range(gi * heads_per_group, (gi + 1) * heads_per_group):
                st[c, h] = prepare(c, h)
    mix_a = sga_ref[...] * _dot(za_ref[...], pa_ref[...])

    for pr in pairs:
        d = st[pr]
        d["p"] = (d["qk"] * d["decay"]).astype(BF16)
        d["g"] = jnp.where(strict, -(d["kk"] * d["decay"]), 0.0)
    for m in masks:
        for pr in pairs:
            d = st[pr]
            g = d["g"]
            d["g"] = g + _dot(g.astype(BF16), (g * m).astype(BF16))
    for pr in pairs:
        d = st[pr]
        tinv = (eye + d["g"]).astype(BF16)
        d["u"] = _dot(tinv, d["vb"])
        d["w"] = _dot(tinv, d["kbg"]).astype(BF16)

    for c in range(nblk):
        r0 = c * BLK
        cur = {}
        for h in range(B_HEADS):
            s_old = s_ref[h]
            s16 = s_old.astype(BF16)
            cur[h] = (s_old, _dot(st[c, h]["w"], s16), _dot(st[c, h]["qg"], s16))
        for h in range(B_HEADS):
            d = st[c, h]
            s_old, ws, qs = cur[h]
            v16 = (d["u"] - ws).astype(BF16)
            cur[h] = (s_old, qs + _dot(d["p"], v16), _dot(d["kgt"], v16))
        for h in range(B_HEADS):
            d = st[c, h]
            s_old, o, ds = cur[h]
            s_ref[h] = s_old * d["gl"] + ds
            o = o * lax.rsqrt(jnp.mean(o * o, axis=-1, keepdims=True) + EPS) * on_ref[...]
            hc = h * B_DV
            zb_ref[r0:r0 + BLK, hc:hc + B_DV] = (o * sz_ref[r0:r0 + BLK, hc:hc + B_DV]).astype(BF16)

    for xp in xpad_refs:
        xp[0:SUBLANES, :] = xp[ts:ts + SUBLANES, :]

    @pl.when(step == pl.num_programs(1) - 1)
    def _():
        so_ref[...] = s_ref[...]

    mix = mix_a + sgb_ref[...] * _dot(zb_ref[...], pb_ref[...])
    y_ref[...] = x_ref[...] + _dot(mix.astype(BF16), wo_ref[...])


def _mixer(x, conv0, s0, consts, *, valid, want_vn):
    b, l, _ = x.shape
    ts = min(TS_MIXER, l)
    assert l % ts == 0 and ts % BLK == 0
    assert valid == ts or l == ts
    tok = lambda c: pl.BlockSpec((None, ts, c), lambda i, j: (i, j, 0))
    const = lambda a: pl.BlockSpec(a.shape, lambda i, j: (0,) * a.ndim, pipeline_mode=pl.Buffered(1))
    per_seq = lambda a: pl.BlockSpec((None,) + a.shape[1:], lambda i, j: (i,) + (0,) * (a.ndim - 1))
    tile = lambda c, dt: pltpu.VMEM((ts, c), dt)
    out_shape = [jax.ShapeDtypeStruct((b, l, D_MODEL), F32),
                 jax.ShapeDtypeStruct((b, CONV_W - 1, 3 * D_B), F32),
                 jax.ShapeDtypeStruct((b, B_HEADS, B_DK, B_DV), F32)]
    out_specs = [tok(D_MODEL),
                 pl.BlockSpec((None, CONV_W - 1, 3 * D_B), lambda i, j: (i, 0, 0)),
                 pl.BlockSpec((None, B_HEADS, B_DK, B_DV), lambda i, j: (i, 0, 0, 0))]
    scratch = [pltpu.VMEM((B_HEADS, B_DK, B_DV), F32),
               tile(D_A, F32), tile(D_B, F32), tile(BA_COLS, F32), tile(D_MODEL, F32), tile(D_MODEL, F32),
               tile(D_A, BF16), tile(D_B, BF16)]
    scratch += [pltpu.VMEM((ts + SUBLANES, PROJ_CHUNK), F32)] * (3 * D_B // PROJ_CHUNK)
    if want_vn:
        out_shape.append(jax.ShapeDtypeStruct((b, l, D_A), F32))
        out_specs.append(tok(D_A))
    else:
        scratch.insert(0, tile(D_A, F32))
    outs = pl.pallas_call(
        functools.partial(_mixer_kernel, ts=ts, valid=valid),
        out_shape=out_shape,
        grid=(b, l // ts),
        in_specs=[tok(D_MODEL), per_seq(conv0), per_seq(s0)] + [const(a) for a in consts],
        out_specs=out_specs,
        scratch_shapes=scratch,
        compiler_params=pltpu.CompilerParams(dimension_semantics=("arbitrary", "arbitrary"),
                                             vmem_limit_bytes=VMEM_LIMIT),
        name="mixer",
    )(x, conv0, s0, *consts)
    return tuple(outs) if want_vn else tuple(outs) + (None,)


def _ffn_kernel(x_ref, ln2_ref, wup_ref, wdn_ref, fn_ref, o_ref, *, final):
    x = x_ref[...]
    h = _rms(x, ln2_ref[...]).astype(BF16)
    acc = x
    for c in range(D_FF // FFN_CHUNK):
        c0 = c * FFN_CHUNK
        hid = _dot(h, wup_ref[:, c0:c0 + FFN_CHUNK])
        act = jnp.square(jnp.maximum(hid, 0.0)).astype(BF16)
        acc = acc + _dot(act, wdn_ref[c0:c0 + FFN_CHUNK, :])
    o_ref[...] = _rms(acc, fn_ref[...]) if final else acc


def _ffn(x2, ln2, wup, wdn, fnorm, *, final):
    t = x2.shape[0]
    tm = min(TM_FFN, t)
    row = pl.BlockSpec((tm, D_MODEL), lambda i: (i, 0))
    const = lambda a: pl.BlockSpec(a.shape, lambda i: (0, 0), pipeline_mode=pl.Buffered(1))
    return pl.pallas_call(
        functools.partial(_ffn_kernel, final=final),
        out_shape=jax.ShapeDtypeStruct((t, D_MODEL), F32),
        grid=(t // tm,),
        in_specs=[row] + [const(a) for a in (ln2, wup, wdn, fnorm)],
        out_specs=row,
        compiler_params=pltpu.CompilerParams(dimension_semantics=("arbitrary",), vmem_limit_bytes=VMEM_LIMIT),
        name="ffn",
    )(x2, ln2, wup, wdn, fnorm)


def _head_major(a):
    lead = a.shape[:-1]
    return a.reshape(*lead, 3, B_HEADS, B_DK).swapaxes(-3, -2).reshape(*lead, 3 * D_B)


def _section_major(a):
    lead = a.shape[:-1]
    return a.reshape(*lead, B_HEADS, 3, B_DK).swapaxes(-3, -2).reshape(*lead, 3 * D_B)


def _prep_layer(l, ln1, w_in, a_ln_g, a_ln_b, w_s, b_s, conv_w, a_log, dt_bias, o_norm, p_a, p_b, w_o, ln2,
                w_up, w_down):
    w = w_in[l]
    o_qkv = 2 * D_A
    o_z = o_qkv + 3 * D_B
    o_b = o_z + D_B
    o_g = o_b + 2 * B_HEADS
    wba = jnp.zeros((D_MODEL, BA_COLS), F32).at[:, :2 * B_HEADS].set(w[:, o_b:o_g])
    pad_a = lambda v: jnp.zeros((1, BA_COLS), F32).at[0, A_COL0:A_COL0 + B_HEADS].set(v)
    row = lambda v: v.reshape(1, -1)
    mixer_consts = (
        row(ln1[l]), w[:, :o_qkv].astype(BF16), _head_major(w[:, o_qkv:o_z]).astype(BF16),
        w[:, o_z:o_b].astype(BF16),
        wba.astype(BF16), w[:, o_g:].astype(BF16), row(a_ln_g[l]), row(a_ln_b[l]),
        w_s[l], b_s[l].T, _head_major(conv_w[l].T), pad_a(a_log[l]), pad_a(dt_bias[l]), row(o_norm[l]),
        p_a[l].astype(BF16), p_b[l].astype(BF16), w_o[l].astype(BF16))
    ffn_consts = (row(ln2[l]), w_up[l].astype(BF16), w_down[l].astype(BF16))
    return mixer_consts, ffn_consts


def _layer(x, conv0, s0, consts, fnorm, *, valid, final, want_vn):
    b, l, _ = x.shape
    mixer_consts, ffn_consts = consts
    y, conv_state, s_new, vn = _mixer(x, _head_major(conv0), s0, mixer_consts, valid=valid, want_vn=want_vn)
    out = _ffn(y.reshape(b * l, D_MODEL), *ffn_consts, fnorm, final=final)
    return out.reshape(b, l, D_MODEL), _section_major(conv_state), s_new, vn


def kernel(x_prompt, x_sample, state_conv, state_delta, ln1, w_in, a_ln_g, a_ln_b, w_s, b_s, conv_w, a_log, dt_bias,
           o_norm, p_a, p_b, w_o, ln2, w_up, w_down, final_norm):
    depth = w_in.shape[0]
    bp, lp, _ = x_prompt.shape
    bs, ls, _ = x_sample.shape
    ls_pad = -(-ls // BLK) * BLK
    yp = x_prompt
    ys = jnp.pad(x_sample, ((0, 0), (0, ls_pad - ls), (0, 0)))
    zero_conv = jnp.zeros((bp, CONV_W - 1, 3 * D_B), F32)
    zero_delta = jnp.zeros((bp, B_HEADS, B_DK, B_DV), F32)
    fnorm = final_norm.reshape(1, -1)
    conv_p, delta_p, conv_s, delta_s, gv_s = [], [], [], [], []
    for l in range(depth):
        consts = _prep_layer(l, ln1, w_in, a_ln_g, a_ln_b, w_s, b_s, conv_w, a_log, dt_bias, o_norm, p_a, p_b, w_o,
                             ln2, w_up, w_down)
        final = l == depth - 1
        yp, cp, dp, _ = _layer(yp, zero_conv, zero_delta, consts, fnorm, valid=min(TS_MIXER, lp), final=final,
                               want_vn=False)
        ys, cs, ds, vs = _layer(ys, state_conv[l], state_delta[l], consts, fnorm, valid=ls, final=final,
                                want_vn=True)
        conv_p.append(cp); delta_p.append(dp); conv_s.append(cs); delta_s.append(ds); gv_s.append(vs[:, :ls])
    return (yp, ys[:, :ls], jnp.stack(conv_p), jnp.stack(delta_p), jnp.stack(conv_s), jnp.stack(delta_s),
            jnp.stack(gv_s))
```

```python
import functools

import jax
import jax.numpy as jnp
from jax import lax
from jax.experimental import pallas as pl
from jax.experimental.pallas import tpu as pltpu

F32 = jnp.float32
BF16 = jnp.bfloat16

D_MODEL = 1024
D_A = 1024
A_GROUPS = 8
A_GROUP_W = D_A // A_GROUPS
B_HEADS = 8
B_DK = 128
B_DV = 128
D_B = B_HEADS * B_DV
CONV_W = 4
D_FF = 4 * D_MODEL
EPS = 1e-6
GMLP_CHUNK = 128

LANES = 128
SUBLANES = 8
BLK = 128
BA_COLS = LANES
A_COL0 = B_HEADS
NEG_BIG = -1e30
VMEM_LIMIT = 60000 * 1024

TM_FFN = 512
FFN_CHUNK = 1024
TS_MIXER = 256
HEAD_COLS = B_DK + B_DK + B_DV
PROJ_CHUNK = 2 * HEAD_COLS
SLABS = PROJ_CHUNK // LANES


def _rms(x, g):
    return x * lax.rsqrt(jnp.mean(x * x, axis=-1, keepdims=True) + EPS) * g


def _gelu(x):
    return 0.5 * x * (1.0 + lax.erf(x * 0.7071067811865476))


def _sigmoid(x):
    return jax.nn.sigmoid(x)


def _silu(x):
    return x * _sigmoid(x)


def _dot(a, b):
    return jnp.dot(a, b, preferred_element_type=F32)


def _dot_nt(a, b):
    return lax.dot_general(a, b, (((1,), (1,)), ((), ())), preferred_element_type=F32)


def _split3(v):
    hi = v.astype(BF16)
    r1 = v - hi.astype(F32)
    mid = r1.astype(BF16)
    lo = (r1 - mid.astype(F32)).astype(BF16)
    return hi, mid, lo


def _level_masks():
    row = lax.broadcasted_iota(jnp.int32, (BLK, BLK), 0)
    col = lax.broadcasted_iota(jnp.int32, (BLK, BLK), 1)
    x = row ^ col
    masks = []
    b, k = 1, 0
    while b < BLK:
        masks.append(jnp.where(((x >> k) == 1) & ((row & b) != 0), 1.0, 0.0).astype(F32))
        b, k = 2 * b, k + 1
    return row, col, masks


def _mixer_kernel(x_ref, conv0_ref, s0_ref,
                  ln1_ref, wuv_ref, wqkv_ref, wz_ref, wba_ref, wg_ref, alg_ref, alb_ref,
                  ws_ref, bst_ref, cw_ref, alog_ref, dtb_ref, on_ref, pa_ref, pb_ref, wo_ref,
                  y_ref, convo_ref, so_ref, *rest, ts, valid):
    vn_ref, s_ref, ug_ref, sz_ref, ba_ref, sga_ref, sgb_ref, za_ref, zb_ref, *xpad_refs = rest
    step = pl.program_id(1)
    nblk = ts // BLK
    tail = slice(SUBLANES - (CONV_W - 1), SUBLANES)

    @pl.when(step == 0)
    def _():
        for i, xp in enumerate(xpad_refs):
            xp[:, 0:SUBLANES, :] = jnp.zeros((SLABS, SUBLANES, LANES), F32)
            for s in range(SLABS):
                c0 = i * PROJ_CHUNK + s * LANES
                xp[s, tail, :] = conv0_ref[:, c0:c0 + LANES]
        s_ref[...] = s0_ref[...]

    xn = _rms(x_ref[...], ln1_ref[...]).astype(BF16)
    ba_ref[...] = _dot(xn, wba_ref[...])
    for i, xp in enumerate(xpad_refs):
        sec = _dot(xn, wqkv_ref[:, i * PROJ_CHUNK:(i + 1) * PROJ_CHUNK])
        for s in range(SLABS):
            xp[s, SUBLANES:SUBLANES + ts, :] = sec[:, s * LANES:(s + 1) * LANES]

    def proj_u():
        ug_ref[...] = _gelu(_dot(xn, wuv_ref[:, :D_A]))

    def proj_v():
        va = _gelu(_dot(xn, wuv_ref[:, D_A:]))
        mu = jnp.mean(va, axis=-1, keepdims=True)
        vc = va - mu
        var = jnp.mean(vc * vc, axis=-1, keepdims=True)
        vn_ref[...] = vc * lax.rsqrt(var + EPS) * alg_ref[...] + alb_ref[...]

    def proj_z():
        sz_ref[...] = _silu(_dot(xn, wz_ref[...]))

    def proj_ga():
        sga_ref[...] = _sigmoid(_dot(xn, wg_ref[:, :D_MODEL]))

    def proj_gb():
        sgb_ref[...] = _sigmoid(_dot(xn, wg_ref[:, D_MODEL:]))

    for i, xp in enumerate(xpad_refs):
        for s in range(SLABS):
            c0 = i * PROJ_CHUNK + s * LANES
            convo_ref[:, c0:c0 + LANES] = xp[s, SUBLANES + valid - (CONV_W - 1):SUBLANES + valid, :]

    def conv_slab(t0, c0):
        xp, s = xpad_refs[c0 // PROJ_CHUNK], (c0 % PROJ_CHUNK) // LANES
        acc = None
        for j in range(CONV_W):
            r0 = t0 + SUBLANES - (CONV_W - 1) + j
            term = xp[s, r0:r0 + BLK, :] * cw_ref[j:j + 1, c0:c0 + LANES]
            acc = term if acc is None else acc + term
        return _silu(acc)

    row, col, masks = _level_masks()
    causal = row >= col
    strict = row > col
    eye = jnp.where(row == col, 1.0, 0.0).astype(F32)
    tri01 = jnp.where(causal, 1.0, 0.0).astype(BF16)

    def spatial_gating():
        gchunk = min(GMLP_CHUNK, ts)
        for c in range(ts // gchunk):
            r0 = c * gchunk
            for g in range(A_GROUPS):
                c0 = g * A_GROUP_W
                w = jnp.where(causal, ws_ref[g], 0.0).astype(BF16)
                s = _dot(w, vn_ref[r0:r0 + gchunk, c0:c0 + A_GROUP_W].astype(BF16)) + bst_ref[:, g:g + 1]
                za_ref[r0:r0 + gchunk, c0:c0 + A_GROUP_W] = (
                    ug_ref[r0:r0 + gchunk, c0:c0 + A_GROUP_W] * s).astype(BF16)

    pairs = [(c, h) for c in range(nblk) for h in range(B_HEADS)]
    per_blk = []
    for c in range(nblk):
        r0 = c * BLK
        ba = ba_ref[r0:r0 + BLK, :]
        beta_c = _sigmoid(ba)
        g_c = -jnp.exp(alog_ref[...]) * jax.nn.softplus(ba + dtb_ref[...])
        if valid < ts:
            live = lax.broadcasted_iota(jnp.int32, (BLK, BA_COLS), 0) + r0 < valid
            beta_c = jnp.where(live, beta_c, 0.0)
            g_c = jnp.where(live, g_c, 0.0)
        gam_c = sum(_dot(tri01, p) for p in _split3(g_c))
        per_blk.append((beta_c, gam_c, gam_c.T))

    def prepare(c, h):
        r0 = c * BLK
        beta_c, gam_c, gam_t = per_blk[c]
        q = conv_slab(r0, h * HEAD_COLS)
        k = conv_slab(r0, h * HEAD_COLS + B_DK)
        v = conv_slab(r0, h * HEAD_COLS + 2 * B_DK)
        q = q * lax.rsqrt(jnp.sum(q * q, axis=-1, keepdims=True) + EPS) * (B_DK ** -0.5)
        k = k * lax.rsqrt(jnp.sum(k * k, axis=-1, keepdims=True) + EPS)
        beta = beta_c[:, h:h + 1]
        gam = jnp.broadcast_to(gam_c[:, A_COL0 + h:A_COL0 + h + 1], (BLK, LANES))
        gam_row = gam_t[A_COL0 + h:A_COL0 + h + 1, :]
        gam_last = gam[BLK - 1:BLK, :]
        eg = jnp.exp(gam)
        kb = k * beta
        k16 = k.astype(BF16)
        return dict(
            decay=jnp.exp(jnp.where(causal, gam - gam_row, NEG_BIG)),
            kk=_dot_nt(kb.astype(BF16), k16), qk=_dot_nt(q.astype(BF16), k16),
            vb=(v * beta).astype(BF16), kbg=(kb * eg).astype(BF16),
            qg=(q * eg).astype(BF16), kgt=(k * jnp.exp(gam_last - gam)).T.astype(BF16),
            gl=jnp.exp(gam_last))

    dense_stages = ((proj_u,), (proj_v,), (spatial_gating, proj_z), (proj_ga, proj_gb))
    heads_per_group = B_HEADS // len(dense_stages)
    st = {}
    for gi, stages in enumerate(dense_stages):
        for stage in stages:
            stage()
        for c in range(nblk):
            for h in range(gi * heads_per_group, (gi + 1) * heads_per_group):
                st[c, h] = prepare(c, h)
    mix_a = sga_ref[...] * _dot(za_ref[...], pa_ref[...])

    for pr in pairs:
        d = st[pr]
        d["p"] = (d["qk"] * d["decay"]).astype(BF16)
        d["g"] = jnp.where(strict, -(d["kk"] * d["decay"]), 0.0)
    for m in masks:
        for pr in pairs:
            d = st[pr]
            g = d["g"]
            d["g"] = g + _dot(g.astype(BF16), (g * m).astype(BF16))
    for pr in pairs:
        d = st[pr]
        tinv = (eye + d["g"]).astype(BF16)
        d["u"] = _dot(tinv, d["vb"])
        d["w"] = _dot(tinv, d["kbg"]).astype(BF16)

    def merge(c):
        rows = slice(c * BLK, (c + 1) * BLK)
        mix = mix_a[rows] + sgb_ref[rows, :] * _dot(zb_ref[rows, :], pb_ref[...])
        y_ref[rows, :] = x_ref[rows, :] + _dot(mix.astype(BF16), wo_ref[...])

    for c in range(nblk):
        r0 = c * BLK
        cur = {}
        for h in range(B_HEADS):
            s_old = s_ref[h]
            s16 = s_old.astype(BF16)
            cur[h] = (s_old, _dot(st[c, h]["w"], s16), _dot(st[c, h]["qg"], s16))
        if c > 0:
            merge(c - 1)
        for h in range(B_HEADS):
            d = st[c, h]
            s_old, ws, qs = cur[h]
            v16 = (d["u"] - ws).astype(BF16)
            cur[h] = (s_old, qs + _dot(d["p"], v16), _dot(d["kgt"], v16))
        for h in range(B_HEADS):
            d = st[c, h]
            s_old, o, ds = cur[h]
            s_ref[h] = s_old * d["gl"] + ds
            o = o * lax.rsqrt(jnp.mean(o * o, axis=-1, keepdims=True) + EPS) * on_ref[...]
            hc = h * B_DV
            zb_ref[r0:r0 + BLK, hc:hc + B_DV] = (o * sz_ref[r0:r0 + BLK, hc:hc + B_DV]).astype(BF16)

    merge(nblk - 1)

    for xp in xpad_refs:
        xp[:, 0:SUBLANES, :] = xp[:, ts:ts + SUBLANES, :]

    @pl.when(step == pl.num_programs(1) - 1)
    def _():
        so_ref[...] = s_ref[...]


def _mixer(x, conv0, s0, consts, *, valid, want_vn):
    b, l, _ = x.shape
    ts = min(TS_MIXER, l)
    assert l % ts == 0 and ts % BLK == 0
    assert valid == ts or l == ts
    tok = lambda c: pl.BlockSpec((None, ts, c), lambda i, j: (i, j, 0))
    const = lambda a: pl.BlockSpec(a.shape, lambda i, j: (0,) * a.ndim, pipeline_mode=pl.Buffered(1))
    per_seq = lambda a: pl.BlockSpec((None,) + a.shape[1:], lambda i, j: (i,) + (0,) * (a.ndim - 1))
    tile = lambda c, dt: pltpu.VMEM((ts, c), dt)
    out_shape = [jax.ShapeDtypeStruct((b, l, D_MODEL), F32),
                 jax.ShapeDtypeStruct((b, CONV_W - 1, 3 * D_B), F32),
                 jax.ShapeDtypeStruct((b, B_HEADS, B_DK, B_DV), F32)]
    out_specs = [tok(D_MODEL),
                 pl.BlockSpec((None, CONV_W - 1, 3 * D_B), lambda i, j: (i, 0, 0)),
                 pl.BlockSpec((None, B_HEADS, B_DK, B_DV), lambda i, j: (i, 0, 0, 0))]
    scratch = [pltpu.VMEM((B_HEADS, B_DK, B_DV), F32),
               tile(D_A, F32), tile(D_B, F32), tile(BA_COLS, F32), tile(D_MODEL, F32), tile(D_MODEL, F32),
               tile(D_A, BF16), tile(D_B, BF16)]
    scratch += [pltpu.VMEM((SLABS, ts + SUBLANES, LANES), F32)] * (3 * D_B // PROJ_CHUNK)
    if want_vn:
        out_shape.append(jax.ShapeDtypeStruct((b, l, D_A), F32))
        out_specs.append(tok(D_A))
    else:
        scratch.insert(0, tile(D_A, F32))
    outs = pl.pallas_call(
        functools.partial(_mixer_kernel, ts=ts, valid=valid),
        out_shape=out_shape,
        grid=(b, l // ts),
        in_specs=[tok(D_MODEL), per_seq(conv0), per_seq(s0)] + [const(a) for a in consts],
        out_specs=out_specs,
        scratch_shapes=scratch,
        compiler_params=pltpu.CompilerParams(dimension_semantics=("arbitrary", "arbitrary"),
                                             vmem_limit_bytes=VMEM_LIMIT),
        name="mixer",
    )(x, conv0, s0, *consts)
    return tuple(outs) if want_vn else tuple(outs) + (None,)


def _ffn_kernel(x_ref, ln2_ref, wup_ref, wdn_ref, fn_ref, o_ref, *, final):
    x = x_ref[...]
    h = _rms(x, ln2_ref[...]).astype(BF16)
    acc = x
    for c in range(D_FF // FFN_CHUNK):
        c0 = c * FFN_CHUNK
        hid = _dot(h, wup_ref[:, c0:c0 + FFN_CHUNK])
        act = jnp.square(jnp.maximum(hid, 0.0)).astype(BF16)
        acc = acc + _dot(act, wdn_ref[c0:c0 + FFN_CHUNK, :])
    o_ref[...] = _rms(acc, fn_ref[...]) if final else acc


def _ffn(x2, ln2, wup, wdn, fnorm, *, final):
    t = x2.shape[0]
    tm = min(TM_FFN, t)
    row = pl.BlockSpec((tm, D_MODEL), lambda i: (i, 0))
    const = lambda a: pl.BlockSpec(a.shape, lambda i: (0, 0), pipeline_mode=pl.Buffered(1))
    return pl.pallas_call(
        functools.partial(_ffn_kernel, final=final),
        out_shape=jax.ShapeDtypeStruct((t, D_MODEL), F32),
        grid=(t // tm,),
        in_specs=[row] + [const(a) for a in (ln2, wup, wdn, fnorm)],
        out_specs=row,
        compiler_params=pltpu.CompilerParams(dimension_semantics=("arbitrary",), vmem_limit_bytes=VMEM_LIMIT),
        name="ffn",
    )(x2, ln2, wup, wdn, fnorm)


def _head_major(a):
    lead = a.shape[:-1]
    return a.reshape(*lead, 3, B_HEADS, B_DK).swapaxes(-3, -2).reshape(*lead, 3 * D_B)


def _section_major(a):
    lead = a.shape[:-1]
    return a.reshape(*lead, B_HEADS, 3, B_DK).swapaxes(-3, -2).reshape(*lead, 3 * D_B)


def _prep_layer(l, ln1, w_in, a_ln_g, a_ln_b, w_s, b_s, conv_w, a_log, dt_bias, o_norm, p_a, p_b, w_o, ln2,
                w_up, w_down):
    w = w_in[l]
    o_qkv = 2 * D_A
    o_z = o_qkv + 3 * D_B
    o_b = o_z + D_B
    o_g = o_b + 2 * B_HEADS
    wba = jnp.zeros((D_MODEL, BA_COLS), F32).at[:, :2 * B_HEADS].set(w[:, o_b:o_g])
    pad_a = lambda v: jnp.zeros((1, BA_COLS), F32).at[0, A_COL0:A_COL0 + B_HEADS].set(v)
    row = lambda v: v.reshape(1, -1)
    mixer_consts = (
        row(ln1[l]), w[:, :o_qkv].astype(BF16), _head_major(w[:, o_qkv:o_z]).astype(BF16),
        w[:, o_z:o_b].astype(BF16),
        wba.astype(BF16), w[:, o_g:].astype(BF16), row(a_ln_g[l]), row(a_ln_b[l]),
        w_s[l], b_s[l].T, _head_major(conv_w[l].T), pad_a(a_log[l]), pad_a(dt_bias[l]), row(o_norm[l]),
        p_a[l].astype(BF16), p_b[l].astype(BF16), w_o[l].astype(BF16))
    ffn_consts = (row(ln2[l]), w_up[l].astype(BF16), w_down[l].astype(BF16))
    return mixer_consts, ffn_consts


def _layer(x, conv0, s0, consts, fnorm, *, valid, final, want_vn):
    b, l, _ = x.shape
    mixer_consts, ffn_consts = consts
    y, conv_state, s_new, vn = _mixer(x, _head_major(conv0), s0, mixer_consts, valid=valid, want_vn=want_vn)
    out = _ffn(y.reshape(b * l, D_MODEL), *ffn_consts, fnorm, final=final)
    return out.reshape(b, l, D_MODEL), _section_major(conv_state), s_new, vn


def kernel(x_prompt, x_sample, state_conv, state_delta, ln1, w_in, a_ln_g, a_ln_b, w_s, b_s, conv_w, a_log, dt_bias,
           o_norm, p_a, p_b, w_o, ln2, w_up, w_down, final_norm):
    depth = w_in.shape[0]
    bp, lp, _ = x_prompt.shape
    bs, ls, _ = x_sample.shape
    ls_pad = -(-ls // BLK) * BLK
    yp = x_prompt
    ys = jnp.pad(x_sample, ((0, 0), (0, ls_pad - ls), (0, 0)))
    zero_conv = jnp.zeros((bp, CONV_W - 1, 3 * D_B), F32)
    zero_delta = jnp.zeros((bp, B_HEADS, B_DK, B_DV), F32)
    fnorm = final_norm.reshape(1, -1)
    conv_p, delta_p, conv_s, delta_s, gv_s = [], [], [], [], []
    for l in range(depth):
        consts = _prep_layer(l, ln1, w_in, a_ln_g, a_ln_b, w_s, b_s, conv_w, a_log, dt_bias, o_norm, p_a, p_b, w_o,
                             ln2, w_up, w_down)
        final = l == depth - 1
        yp, cp, dp, _ = _layer(yp, zero_conv, zero_delta, consts, fnorm, valid=min(TS_MIXER, lp), final=final,
                               want_vn=False)
        ys, cs, ds, vs = _layer(ys, state_conv[l], state_delta[l], consts, fnorm, valid=ls, final=final,
                                want_vn=True)
        conv_p.append(cp); delta_p.append(dp); conv_s.append(cs); delta_s.append(ds); gv_s.append(vs[:, :ls])
    return (yp, ys[:, :ls], jnp.stack(conv_p), jnp.stack(delta_p), jnp.stack(conv_s), jnp.stack(delta_s),
            jnp.stack(gv_s))
```

```python
import functools

import jax
import jax.numpy as jnp
from jax import lax
from jax.experimental import pallas as pl
from jax.experimental.pallas import tpu as pltpu

F32 = jnp.float32
BF16 = jnp.bfloat16

D_MODEL = 1024
D_A = 1024
A_GROUPS = 8
A_GROUP_W = D_A // A_GROUPS
B_HEADS = 8
B_DK = 128
B_DV = 128
D_B = B_HEADS * B_DV
CONV_W = 4
D_FF = 4 * D_MODEL
EPS = 1e-6
GMLP_CHUNK = 128

LANES = 128
SUBLANES = 8
BLK = 128
BA_COLS = LANES
A_COL0 = B_HEADS
NEG_BIG = -1e30
VMEM_LIMIT = 60000 * 1024

TM_FFN = 1024
FFN_CHUNK = 1024
TS_MIXER = 256
HEAD_COLS = B_DK + B_DK + B_DV
PROJ_CHUNK = 2 * HEAD_COLS
SLABS = PROJ_CHUNK // LANES


def _rms(x, g):
    return x * lax.rsqrt(jnp.mean(x * x, axis=-1, keepdims=True) + EPS) * g


def _gelu(x):
    return 0.5 * x * (1.0 + lax.erf(x * 0.7071067811865476))


def _sigmoid(x):
    return jax.nn.sigmoid(x)


def _silu(x):
    return x * _sigmoid(x)


def _dot(a, b):
    return jnp.dot(a, b, preferred_element_type=F32)


def _dot_nt(a, b):
    return lax.dot_general(a, b, (((1,), (1,)), ((), ())), preferred_element_type=F32)


def _split3(v):
    hi = v.astype(BF16)
    r1 = v - hi.astype(F32)
    mid = r1.astype(BF16)
    lo = (r1 - mid.astype(F32)).astype(BF16)
    return hi, mid, lo


def _level_masks():
    row = lax.broadcasted_iota(jnp.int32, (BLK, BLK), 0)
    col = lax.broadcasted_iota(jnp.int32, (BLK, BLK), 1)
    x = row ^ col
    masks = []
    b, k = 1, 0
    while b < BLK:
        masks.append(jnp.where(((x >> k) == 1) & ((row & b) != 0), 1.0, 0.0).astype(F32))
        b, k = 2 * b, k + 1
    return row, col, masks


def _mixer_kernel(x_ref, conv0_ref, s0_ref,
                  ln1_ref, wuv_ref, wqkv_ref, wz_ref, wba_ref, wg_ref, alg_ref, alb_ref,
                  ws_ref, bst_ref, cw_ref, alog_ref, dtb_ref, on_ref, pa_ref, pb_ref, wo_ref,
                  y_ref, convo_ref, so_ref, *rest, ts, valid):
    vn_ref, s_ref, ug_ref, sz_ref, ba_ref, sga_ref, sgb_ref, za_ref, zb_ref, *xpad_refs = rest
    step = pl.program_id(1)
    nblk = ts // BLK
    tail = slice(SUBLANES - (CONV_W - 1), SUBLANES)

    def state_cols(i, s):
        head, part = divmod(i * SLABS + s, HEAD_COLS // LANES)
        c0 = part * D_B + head * B_DK
        return slice(c0, c0 + LANES)

    @pl.when(step == 0)
    def _():
        for i, xp in enumerate(xpad_refs):
            xp[:, 0:SUBLANES, :] = jnp.zeros((SLABS, SUBLANES, LANES), F32)
            for s in range(SLABS):
                xp[s, tail, :] = conv0_ref[:, state_cols(i, s)]
        s_ref[...] = s0_ref[...]

    xn = _rms(x_ref[...], ln1_ref[...]).astype(BF16)
    ba_ref[...] = _dot(xn, wba_ref[...])
    for i, xp in enumerate(xpad_refs):
        sec = _dot(xn, wqkv_ref[:, i * PROJ_CHUNK:(i + 1) * PROJ_CHUNK])
        for s in range(SLABS):
            xp[s, SUBLANES:SUBLANES + ts, :] = sec[:, s * LANES:(s + 1) * LANES]

    def proj_u():
        ug_ref[...] = _gelu(_dot(xn, wuv_ref[:, :D_A]))

    def proj_v():
        va = _gelu(_dot(xn, wuv_ref[:, D_A:]))
        mu = jnp.mean(va, axis=-1, keepdims=True)
        vc = va - mu
        var = jnp.mean(vc * vc, axis=-1, keepdims=True)
        vn_ref[...] = vc * lax.rsqrt(var + EPS) * alg_ref[...] + alb_ref[...]

    def proj_z():
        sz_ref[...] = _silu(_dot(xn, wz_ref[...]))

    def proj_ga():
        sga_ref[...] = _sigmoid(_dot(xn, wg_ref[:, :D_MODEL]))

    def proj_gb():
        sgb_ref[...] = _sigmoid(_dot(xn, wg_ref[:, D_MODEL:]))

    for i, xp in enumerate(xpad_refs):
        for s in range(SLABS):
            convo_ref[:, state_cols(i, s)] = xp[s, SUBLANES + valid - (CONV_W - 1):SUBLANES + valid, :]

    def conv_slab(t0, c0):
        xp, s = xpad_refs[c0 // PROJ_CHUNK], (c0 % PROJ_CHUNK) // LANES
        acc = None
        for j in range(CONV_W):
            r0 = t0 + SUBLANES - (CONV_W - 1) + j
            term = xp[s, r0:r0 + BLK, :] * cw_ref[j:j + 1, c0:c0 + LANES]
            acc = term if acc is None else acc + term
        return _silu(acc)

    row, col, masks = _level_masks()
    causal = row >= col
    strict = row > col
    eye = jnp.where(row == col, 1.0, 0.0).astype(F32)
    tri01 = jnp.where(causal, 1.0, 0.0).astype(BF16)

    def spatial_gating():
        gchunk = min(GMLP_CHUNK, ts)
        for c in range(ts // gchunk):
            r0 = c * gchunk
            for g in range(A_GROUPS):
                c0 = g * A_GROUP_W
                w = jnp.where(causal, ws_ref[g], 0.0).astype(BF16)
                s = _dot(w, vn_ref[r0:r0 + gchunk, c0:c0 + A_GROUP_W].astype(BF16)) + bst_ref[:, g:g + 1]
                za_ref[r0:r0 + gchunk, c0:c0 + A_GROUP_W] = (
                    ug_ref[r0:r0 + gchunk, c0:c0 + A_GROUP_W] * s).astype(BF16)

    pairs = [(c, h) for c in range(nblk) for h in range(B_HEADS)]
    per_blk = []
    for c in range(nblk):
        r0 = c * BLK
        ba = ba_ref[r0:r0 + BLK, :]
        beta_c = _sigmoid(ba)
        g_c = -jnp.exp(alog_ref[...]) * jax.nn.softplus(ba + dtb_ref[...])
        if valid < ts:
            live = lax.broadcasted_iota(jnp.int32, (BLK, BA_COLS), 0) + r0 < valid
            beta_c = jnp.where(live, beta_c, 0.0)
            g_c = jnp.where(live, g_c, 0.0)
        gam_c = sum(_dot(tri01, p) for p in _split3(g_c))
        per_blk.append((beta_c, gam_c, gam_c.T))

    def prepare(c, h):
        r0 = c * BLK
        beta_c, gam_c, gam_t = per_blk[c]
        q = conv_slab(r0, h * HEAD_COLS)
        k = conv_slab(r0, h * HEAD_COLS + B_DK)
        v = conv_slab(r0, h * HEAD_COLS + 2 * B_DK)
        q = q * lax.rsqrt(jnp.sum(q * q, axis=-1, keepdims=True) + EPS) * (B_DK ** -0.5)
        k = k * lax.rsqrt(jnp.sum(k * k, axis=-1, keepdims=True) + EPS)
        beta = beta_c[:, h:h + 1]
        gam = jnp.broadcast_to(gam_c[:, A_COL0 + h:A_COL0 + h + 1], (BLK, LANES))
        gam_row = gam_t[A_COL0 + h:A_COL0 + h + 1, :]
        gam_last = gam[BLK - 1:BLK, :]
        eg = jnp.exp(gam)
        kb = k * beta
        kq = _dot_nt(jnp.concatenate([kb.astype(BF16), q.astype(BF16)], axis=0), k.astype(BF16))
        return dict(
            decay=jnp.exp(jnp.where(causal, gam - gam_row, NEG_BIG)),
            kk=kq[:BLK], qk=kq[BLK:],
            vb=(v * beta).astype(BF16), kbg=(kb * eg).astype(BF16),
            qg=(q * eg).astype(BF16), kgt=(k * jnp.exp(gam_last - gam)).T.astype(BF16),
            gl=jnp.exp(gam_last))

    dense_stages = ((proj_u,), (proj_v,), (proj_z,), (proj_ga, proj_gb))
    heads_per_group = B_HEADS // len(dense_stages)
    st = {}
    for gi, stages in enumerate(dense_stages):
        for stage in stages:
            stage()
        for c in range(nblk):
            for h in range(gi * heads_per_group, (gi + 1) * heads_per_group):
                st[c, h] = prepare(c, h)
    spatial_gating()
    mix_a = sga_ref[...] * _dot(za_ref[...], pa_ref[...])

    for pr in pairs:
        d = st[pr]
        d["pk"] = jnp.concatenate([(d["qk"] * d["decay"]).astype(BF16), d["kgt"]], axis=0)
        d["g"] = jnp.where(strict, -(d["kk"] * d["decay"]), 0.0)
    for m in masks:
        for pr in pairs:
            d = st[pr]
            g = d["g"]
            d["g"] = g + _dot(g.astype(BF16), (g * m).astype(BF16))
    for pr in pairs:
        d = st[pr]
        tinv = (eye + d["g"]).astype(BF16)
        d["u"] = _dot(tinv, d["vb"])
        d["wq"] = jnp.concatenate([_dot(tinv, d["kbg"]).astype(BF16), d["qg"]], axis=0)

    def merge(c):
        rows = slice(c * BLK, (c + 1) * BLK)
        mix = mix_a[rows] + sgb_ref[rows, :] * _dot(zb_ref[rows, :], pb_ref[...])
        y_ref[rows, :] = x_ref[rows, :] + _dot(mix.astype(BF16), wo_ref[...])

    for c in range(nblk):
        r0 = c * BLK
        cur = {}
        for h in range(B_HEADS):
            s_old = s_ref[h]
            wqs = _dot(st[c, h]["wq"], s_old.astype(BF16))
            cur[h] = (s_old, wqs[:BLK], wqs[BLK:])
        if c > 0:
            merge(c - 1)
        for h in range(B_HEADS):
            d = st[c, h]
            s_old, ws, qs = cur[h]
            pkv = _dot(d["pk"], (d["u"] - ws).astype(BF16))
            cur[h] = (s_old, qs + pkv[:BLK], pkv[BLK:])
        for h in range(B_HEADS):
            d = st[c, h]
            s_old, o, ds = cur[h]
            s_ref[h] = s_old * d["gl"] + ds
            o = o * lax.rsqrt(jnp.mean(o * o, axis=-1, keepdims=True) + EPS) * on_ref[...]
            hc = h * B_DV
            zb_ref[r0:r0 + BLK, hc:hc + B_DV] = (o * sz_ref[r0:r0 + BLK, hc:hc + B_DV]).astype(BF16)

    merge(nblk - 1)

    for xp in xpad_refs:
        xp[:, 0:SUBLANES, :] = xp[:, ts:ts + SUBLANES, :]

    @pl.when(step == pl.num_programs(1) - 1)
    def _():
        so_ref[...] = s_ref[...]


def _mixer(x, conv0, s0, consts, *, valid, want_vn):
    b, l, _ = x.shape
    ts = min(TS_MIXER, l)
    assert l % ts == 0 and ts % BLK == 0
    assert valid == ts or l == ts
    tok = lambda c: pl.BlockSpec((None, ts, c), lambda i, j: (i, j, 0))
    const = lambda a: pl.BlockSpec(a.shape, lambda i, j: (0,) * a.ndim, pipeline_mode=pl.Buffered(1))
    per_seq = lambda a: pl.BlockSpec((None,) + a.shape[1:], lambda i, j: (i,) + (0,) * (a.ndim - 1))
    tile = lambda c, dt: pltpu.VMEM((ts, c), dt)
    out_shape = [jax.ShapeDtypeStruct((b, l, D_MODEL), F32),
                 jax.ShapeDtypeStruct((b, CONV_W - 1, 3 * D_B), F32),
                 jax.ShapeDtypeStruct((b, B_HEADS, B_DK, B_DV), F32)]
    out_specs = [tok(D_MODEL),
                 pl.BlockSpec((None, CONV_W - 1, 3 * D_B), lambda i, j: (i, 0, 0)),
                 pl.BlockSpec((None, B_HEADS, B_DK, B_DV), lambda i, j: (i, 0, 0, 0))]
    scratch = [pltpu.VMEM((B_HEADS, B_DK, B_DV), F32),
               tile(D_A, F32), tile(D_B, F32), tile(BA_COLS, F32), tile(D_MODEL, F32), tile(D_MODEL, F32),
               tile(D_A, BF16), tile(D_B, BF16)]
    scratch += [pltpu.VMEM((SLABS, ts + SUBLANES, LANES), F32)] * (3 * D_B // PROJ_CHUNK)
    if want_vn:
        out_shape.append(jax.ShapeDtypeStruct((b, l, D_A), F32))
        out_specs.append(tok(D_A))
    else:
        scratch.insert(0, tile(D_A, F32))
    outs = pl.pallas_call(
        functools.partial(_mixer_kernel, ts=ts, valid=valid),
        out_shape=out_shape,
        grid=(b, l // ts),
        in_specs=[tok(D_MODEL), per_seq(conv0), per_seq(s0)] + [const(a) for a in consts],
        out_specs=out_specs,
        scratch_shapes=scratch,
        compiler_params=pltpu.CompilerParams(dimension_semantics=("arbitrary", "arbitrary"),
                                             vmem_limit_bytes=VMEM_LIMIT),
        name="mixer",
    )(x, conv0, s0, *consts)
    return tuple(outs) if want_vn else tuple(outs) + (None,)


def _ffn_kernel(x_ref, ln2_ref, wup_ref, wdn_ref, fn_ref, o_ref, *, final):
    x = x_ref[...]
    h = _rms(x, ln2_ref[...]).astype(BF16)
    acc = x
    for c in range(D_FF // FFN_CHUNK):
        c0 = c * FFN_CHUNK
        hid = _dot(h, wup_ref[:, c0:c0 + FFN_CHUNK])
        act = jnp.square(jnp.maximum(hid, 0.0)).astype(BF16)
        acc = acc + _dot(act, wdn_ref[c0:c0 + FFN_CHUNK, :])
    o_ref[...] = _rms(acc, fn_ref[...]) if final else acc


def _ffn(x2, ln2, wup, wdn, fnorm, *, final):
    t = x2.shape[0]
    tm = min(TM_FFN, t)
    row = pl.BlockSpec((tm, D_MODEL), lambda i: (i, 0))
    const = lambda a: pl.BlockSpec(a.shape, lambda i: (0, 0), pipeline_mode=pl.Buffered(1))
    return pl.pallas_call(
        functools.partial(_ffn_kernel, final=final),
        out_shape=jax.ShapeDtypeStruct((t, D_MODEL), F32),
        grid=(t // tm,),
        in_specs=[row] + [const(a) for a in (ln2, wup, wdn, fnorm)],
        out_specs=row,
        compiler_params=pltpu.CompilerParams(dimension_semantics=("arbitrary",), vmem_limit_bytes=VMEM_LIMIT),
        name="ffn",
    )(x2, ln2, wup, wdn, fnorm)


def _head_major(a):
    lead = a.shape[:-1]
    return a.reshape(*lead, 3, B_HEADS, B_DK).swapaxes(-3, -2).reshape(*lead, 3 * D_B)


def _prep_layer(l, ln1, w_in, a_ln_g, a_ln_b, w_s, b_s, conv_w, a_log, dt_bias, o_norm, p_a, p_b, w_o, ln2,
                w_up, w_down):
    w = w_in[l]
    o_qkv = 2 * D_A
    o_z = o_qkv + 3 * D_B
    o_b = o_z + D_B
    o_g = o_b + 2 * B_HEADS
    wba = jnp.zeros((D_MODEL, BA_COLS), F32).at[:, :2 * B_HEADS].set(w[:, o_b:o_g])
    pad_a = lambda v: jnp.zeros((1, BA_COLS), F32).at[0, A_COL0:A_COL0 + B_HEADS].set(v)
    row = lambda v: v.reshape(1, -1)
    mixer_consts = (
        row(ln1[l]), w[:, :o_qkv].astype(BF16), _head_major(w[:, o_qkv:o_z]).astype(BF16),
        w[:, o_z:o_b].astype(BF16),
        wba.astype(BF16), w[:, o_g:].astype(BF16), row(a_ln_g[l]), row(a_ln_b[l]),
        w_s[l], b_s[l].T, _head_major(conv_w[l].T), pad_a(a_log[l]), pad_a(dt_bias[l]), row(o_norm[l]),
        p_a[l].astype(BF16), p_b[l].astype(BF16), w_o[l].astype(BF16))
    ffn_consts = (row(ln2[l]), w_up[l].astype(BF16), w_down[l].astype(BF16))
    return mixer_consts, ffn_consts


def _layer(x, conv0, s0, consts, fnorm, *, valid, final, want_vn):
    b, l, _ = x.shape
    mixer_consts, ffn_consts = consts
    y, conv_state, s_new, vn = _mixer(x, conv0, s0, mixer_consts, valid=valid, want_vn=want_vn)
    out = _ffn(y.reshape(b * l, D_MODEL), *ffn_consts, fnorm, final=final)
    return out.reshape(b, l, D_MODEL), conv_state, s_new, vn


def kernel(x_prompt, x_sample, state_conv, state_delta, ln1, w_in, a_ln_g, a_ln_b, w_s, b_s, conv_w, a_log, dt_bias,
           o_norm, p_a, p_b, w_o, ln2, w_up, w_down, final_norm):
    depth = w_in.shape[0]
    bp, lp, _ = x_prompt.shape
    bs, ls, _ = x_sample.shape
    ls_pad = -(-ls // BLK) * BLK
    yp = x_prompt
    ys = jnp.pad(x_sample, ((0, 0), (0, ls_pad - ls), (0, 0)))
    zero_conv = jnp.zeros((bp, CONV_W - 1, 3 * D_B), F32)
    zero_delta = jnp.zeros((bp, B_HEADS, B_DK, B_DV), F32)
    fnorm = final_norm.reshape(1, -1)
    conv_p, delta_p, conv_s, delta_s, gv_s = [], [], [], [], []
    for l in range(depth):
        consts = _prep_layer(l, ln1, w_in, a_ln_g, a_ln_b, w_s, b_s, conv_w, a_log, dt_bias, o_norm, p_a, p_b, w_o,
                             ln2, w_up, w_down)
        final = l == depth - 1
        yp, cp, dp, _ = _layer(yp, zero_conv, zero_delta, consts, fnorm, valid=min(TS_MIXER, lp), final=final,
                               want_vn=False)
        ys, cs, ds, vs = _layer(ys, state_conv[l], state_delta[l], consts, fnorm, valid=ls, final=final,
                                want_vn=True)
        conv_p.append(cp); delta_p.append(dp); conv_s.append(cs); delta_s.append(ds); gv_s.append(vs[:, :ls])
    return (yp, ys[:, :ls], jnp.stack(conv_p), jnp.stack(delta_p), jnp.stack(conv_s), jnp.stack(delta_s),
            jnp.stack(gv_s))
```

```python
import functools

import jax
import jax.numpy as jnp
from jax import lax
from jax.experimental import pallas as pl
from jax.experimental.pallas import tpu as pltpu

F32 = jnp.float32
BF16 = jnp.bfloat16

D_MODEL = 1024
D_A = 1024
A_GROUPS = 8
A_GROUP_W = D_A // A_GROUPS
B_HEADS = 8
B_DK = 128
B_DV = 128
D_B = B_HEADS * B_DV
CONV_W = 4
D_FF = 4 * D_MODEL
EPS = 1e-6
GMLP_CHUNK = 128

LANES = 128
SUBLANES = 8
BLK = 128
BA_COLS = LANES
A_COL0 = B_HEADS
NEG_BIG = -1e30
VMEM_LIMIT = 60000 * 1024

TM_FFN = 1024
FFN_CHUNK = 1024
TS_MIXER = 256
HEAD_COLS = B_DK + B_DK + B_DV
PROJ_CHUNK = 2 * HEAD_COLS
SLABS = PROJ_CHUNK // LANES


def _rms(x, g):
    return x * lax.rsqrt(jnp.mean(x * x, axis=-1, keepdims=True) + EPS) * g


def _gelu(x):
    half = 0.5 * x
    return half + half * lax.erf(x * 0.7071067811865476)


def _sigmoid(x):
    return jax.nn.sigmoid(x)


def _silu(x):
    return x * _sigmoid(x)


def _dot(a, b):
    return jnp.dot(a, b, preferred_element_type=F32)


def _dot_nt(a, b):
    return lax.dot_general(a, b, (((1,), (1,)), ((), ())), preferred_element_type=F32)


def _split3(v):
    hi = v.astype(BF16)
    r1 = v - hi.astype(F32)
    mid = r1.astype(BF16)
    lo = (r1 - mid.astype(F32)).astype(BF16)
    return hi, mid, lo


def _level_masks():
    row = lax.broadcasted_iota(jnp.int32, (BLK, BLK), 0)
    col = lax.broadcasted_iota(jnp.int32, (BLK, BLK), 1)
    x = row ^ col
    masks = []
    b, k = 1, 0
    while b < BLK:
        masks.append(jnp.where(((x >> k) == 1) & ((row & b) != 0), 1.0, 0.0).astype(F32))
        b, k = 2 * b, k + 1
    return row, col, masks


def _mixer_kernel(x_ref, conv0_ref, s0_ref,
                  ln1_ref, wuv_ref, wqkv_ref, wz_ref, wba_ref, wg_ref, alg_ref, alb_ref,
                  ws_ref, bst_ref, cw_ref, alog_ref, dtb_ref, on_ref, pa_ref, pb_ref, wo_ref,
                  y_ref, convo_ref, so_ref, *rest, ts, valid):
    vn_ref, s_ref, ug_ref, sz_ref, ba_ref, sga_ref, sgb_ref, za_ref, zb_ref, *xpad_refs = rest
    step = pl.program_id(1)
    nblk = ts // BLK
    tail = slice(SUBLANES - (CONV_W - 1), SUBLANES)

    def state_cols(i, s):
        head, part = divmod(i * SLABS + s, HEAD_COLS // LANES)
        c0 = part * D_B + head * B_DK
        return slice(c0, c0 + LANES)

    @pl.when(step == 0)
    def _():
        for i, xp in enumerate(xpad_refs):
            xp[:, 0:SUBLANES, :] = jnp.zeros((SLABS, SUBLANES, LANES), F32)
            for s in range(SLABS):
                xp[s, tail, :] = conv0_ref[:, state_cols(i, s)]
        s_ref[...] = s0_ref[...]

    xn = _rms(x_ref[...], ln1_ref[...]).astype(BF16)
    ba_ref[...] = _dot(xn, wba_ref[...])
    for i, xp in enumerate(xpad_refs):
        sec = _dot(xn, wqkv_ref[:, i * PROJ_CHUNK:(i + 1) * PROJ_CHUNK])
        for s in range(SLABS):
            xp[s, SUBLANES:SUBLANES + ts, :] = sec[:, s * LANES:(s + 1) * LANES]

    def proj_u():
        ug_ref[...] = _gelu(_dot(xn, wuv_ref[:, :D_A]))

    def proj_v():
        va = _gelu(_dot(xn, wuv_ref[:, D_A:]))
        mu = jnp.mean(va, axis=-1, keepdims=True)
        vc = va - mu
        var = jnp.mean(vc * vc, axis=-1, keepdims=True)
        vn_ref[...] = vc * lax.rsqrt(var + EPS) * alg_ref[...] + alb_ref[...]

    def proj_z():
        sz_ref[...] = _silu(_dot(xn, wz_ref[...]))

    def proj_ga():
        sga_ref[...] = _sigmoid(_dot(xn, wg_ref[:, :D_MODEL]))

    def proj_gb():
        sgb_ref[...] = _sigmoid(_dot(xn, wg_ref[:, D_MODEL:]))

    for i, xp in enumerate(xpad_refs):
        for s in range(SLABS):
            convo_ref[:, state_cols(i, s)] = xp[s, SUBLANES + valid - (CONV_W - 1):SUBLANES + valid, :]

    def conv_slab(t0, c0):
        xp, s = xpad_refs[c0 // PROJ_CHUNK], (c0 % PROJ_CHUNK) // LANES
        acc = None
        for j in range(CONV_W):
            r0 = t0 + SUBLANES - (CONV_W - 1) + j
            term = xp[s, r0:r0 + BLK, :] * cw_ref[j:j + 1, c0:c0 + LANES]
            acc = term if acc is None else acc + term
        return _silu(acc)

    row, col, masks = _level_masks()
    causal = row >= col
    strict = row > col
    eye = jnp.where(row == col, 1.0, 0.0).astype(F32)
    tri01 = jnp.where(causal, 1.0, 0.0).astype(BF16)

    def spatial_gating():
        gchunk = min(GMLP_CHUNK, ts)
        for c in range(ts // gchunk):
            r0 = c * gchunk
            for g in range(A_GROUPS):
                c0 = g * A_GROUP_W
                w = jnp.where(causal, ws_ref[g], 0.0).astype(BF16)
                s = _dot(w, vn_ref[r0:r0 + gchunk, c0:c0 + A_GROUP_W].astype(BF16)) + bst_ref[:, g:g + 1]
                za_ref[r0:r0 + gchunk, c0:c0 + A_GROUP_W] = (
                    ug_ref[r0:r0 + gchunk, c0:c0 + A_GROUP_W] * s).astype(BF16)

    pairs = [(c, h) for c in range(nblk) for h in range(B_HEADS)]
    per_blk = []
    for c in range(nblk):
        r0 = c * BLK
        ba = ba_ref[r0:r0 + BLK, :]
        beta_c = _sigmoid(ba)
        g_c = -jnp.exp(alog_ref[...]) * jax.nn.softplus(ba + dtb_ref[...])
        if valid < ts:
            live = lax.broadcasted_iota(jnp.int32, (BLK, BA_COLS), 0) + r0 < valid
            beta_c = jnp.where(live, beta_c, 0.0)
            g_c = jnp.where(live, g_c, 0.0)
        gam_c = sum(_dot(tri01, p) for p in _split3(g_c))
        per_blk.append((beta_c, gam_c, gam_c.T))

    def prepare(c, h):
        r0 = c * BLK
        beta_c, gam_c, gam_t = per_blk[c]
        q = conv_slab(r0, h * HEAD_COLS)
        k = conv_slab(r0, h * HEAD_COLS + B_DK)
        v = conv_slab(r0, h * HEAD_COLS + 2 * B_DK)
        q = q * lax.rsqrt(jnp.sum(q * q, axis=-1, keepdims=True) + EPS) * (B_DK ** -0.5)
        k = k * lax.rsqrt(jnp.sum(k * k, axis=-1, keepdims=True) + EPS)
        beta = beta_c[:, h:h + 1]
        gam = jnp.broadcast_to(gam_c[:, A_COL0 + h:A_COL0 + h + 1], (BLK, LANES))
        gam_row = gam_t[A_COL0 + h:A_COL0 + h + 1, :]
        gam_last = gam[BLK - 1:BLK, :]
        eg = jnp.exp(gam)
        kb = k * beta
        kq = _dot_nt(jnp.concatenate([kb.astype(BF16), q.astype(BF16)], axis=0), k.astype(BF16))
        decay = jnp.exp(jnp.where(causal, gam - gam_row, NEG_BIG))
        kgt = (k * jnp.exp(gam_last - gam)).T.astype(BF16)
        return dict(
            g=jnp.where(strict, -(kq[:BLK] * decay), 0.0),
            pk=jnp.concatenate([(kq[BLK:] * decay).astype(BF16), kgt], axis=0),
            vb=(v * beta).astype(BF16), kbg=(kb * eg).astype(BF16), qg=(q * eg).astype(BF16),
            gl=jnp.exp(gam_last))

    dense_stages = ((proj_u,), (proj_v,), (proj_z,), (proj_ga, proj_gb))
    heads_per_group = B_HEADS // len(dense_stages)
    st = {}
    for gi, stages in enumerate(dense_stages):
        for stage in stages:
            stage()
        for c in range(nblk):
            for h in range(gi * heads_per_group, (gi + 1) * heads_per_group):
                st[c, h] = prepare(c, h)
    spatial_gating()
    mix_a = sga_ref[...] * _dot(za_ref[...], pa_ref[...])

    for m in masks:
        for pr in pairs:
            d = st[pr]
            g = d["g"]
            d["g"] = g + _dot(g.astype(BF16), (g * m).astype(BF16))
    for pr in pairs:
        d = st[pr]
        tinv = (eye + d["g"]).astype(BF16)
        d["u"] = _dot(tinv, d["vb"])
        d["wq"] = jnp.concatenate([_dot(tinv, d["kbg"]).astype(BF16), d["qg"]], axis=0)

    def merge(c):
        rows = slice(c * BLK, (c + 1) * BLK)
        mix = mix_a[rows] + sgb_ref[rows, :] * _dot(zb_ref[rows, :], pb_ref[...])
        y_ref[rows, :] = x_ref[rows, :] + _dot(mix.astype(BF16), wo_ref[...])

    for c in range(nblk):
        r0 = c * BLK
        cur = {}
        for h in range(B_HEADS):
            s_old = s_ref[h]
            wqs = _dot(st[c, h]["wq"], s_old.astype(BF16))
            cur[h] = (s_old, wqs[:BLK], wqs[BLK:])
        if c > 0:
            merge(c - 1)
        for h in range(B_HEADS):
            d = st[c, h]
            s_old, ws, qs = cur[h]
            pkv = _dot(d["pk"], (d["u"] - ws).astype(BF16))
            cur[h] = (s_old, qs + pkv[:BLK], pkv[BLK:])
        for h in range(B_HEADS):
            d = st[c, h]
            s_old, o, ds = cur[h]
            s_ref[h] = s_old * d["gl"] + ds
            o = o * lax.rsqrt(jnp.mean(o * o, axis=-1, keepdims=True) + EPS) * on_ref[...]
            hc = h * B_DV
            zb_ref[r0:r0 + BLK, hc:hc + B_DV] = (o * sz_ref[r0:r0 + BLK, hc:hc + B_DV]).astype(BF16)

    merge(nblk - 1)

    for xp in xpad_refs:
        xp[:, 0:SUBLANES, :] = xp[:, ts:ts + SUBLANES, :]

    @pl.when(step == pl.num_programs(1) - 1)
    def _():
        so_ref[...] = s_ref[...]


def _mixer(x, conv0, s0, consts, *, valid, want_vn):
    b, l, _ = x.shape
    ts = min(TS_MIXER, l)
    assert l % ts == 0 and ts % BLK == 0
    assert valid == ts or l == ts
    tok = lambda c: pl.BlockSpec((None, ts, c), lambda i, j: (i, j, 0))
    const = lambda a: pl.BlockSpec(a.shape, lambda i, j: (0,) * a.ndim, pipeline_mode=pl.Buffered(1))
    per_seq = lambda a: pl.BlockSpec((None,) + a.shape[1:], lambda i, j: (i,) + (0,) * (a.ndim - 1))
    tile = lambda c, dt: pltpu.VMEM((ts, c), dt)
    out_shape = [jax.ShapeDtypeStruct((b, l, D_MODEL), F32),
                 jax.ShapeDtypeStruct((b, CONV_W - 1, 3 * D_B), F32),
                 jax.ShapeDtypeStruct((b, B_HEADS, B_DK, B_DV), F32)]
    out_specs = [tok(D_MODEL),
                 pl.BlockSpec((None, CONV_W - 1, 3 * D_B), lambda i, j: (i, 0, 0)),
                 pl.BlockSpec((None, B_HEADS, B_DK, B_DV), lambda i, j: (i, 0, 0, 0))]
    scratch = [pltpu.VMEM((B_HEADS, B_DK, B_DV), F32),
               tile(D_A, F32), tile(D_B, F32), tile(BA_COLS, F32), tile(D_MODEL, F32), tile(D_MODEL, F32),
               tile(D_A, BF16), tile(D_B, BF16)]
    scratch += [pltpu.VMEM((SLABS, ts + SUBLANES, LANES), F32)] * (3 * D_B // PROJ_CHUNK)
    if want_vn:
        out_shape.append(jax.ShapeDtypeStruct((b, l, D_A), F32))
        out_specs.append(tok(D_A))
    else:
        scratch.insert(0, tile(D_A, F32))
    outs = pl.pallas_call(
        functools.partial(_mixer_kernel, ts=ts, valid=valid),
        out_shape=out_shape,
        grid=(b, l // ts),
        in_specs=[tok(D_MODEL), per_seq(conv0), per_seq(s0)] + [const(a) for a in consts],
        out_specs=out_specs,
        scratch_shapes=scratch,
        compiler_params=pltpu.CompilerParams(dimension_semantics=("arbitrary", "arbitrary"),
                                             vmem_limit_bytes=VMEM_LIMIT),
        name="mixer",
    )(x, conv0, s0, *consts)
    return tuple(outs) if want_vn else tuple(outs) + (None,)


def _ffn_kernel(x_ref, ln2_ref, wup_ref, wdn_ref, fn_ref, o_ref, *, final):
    x = x_ref[...]
    h = _rms(x, ln2_ref[...]).astype(BF16)
    acc = x
    for c in range(D_FF // FFN_CHUNK):
        c0 = c * FFN_CHUNK
        hid = _dot(h, wup_ref[:, c0:c0 + FFN_CHUNK])
        act = jnp.square(jnp.maximum(hid, 0.0)).astype(BF16)
        acc = acc + _dot(act, wdn_ref[c0:c0 + FFN_CHUNK, :])
    o_ref[...] = _rms(acc, fn_ref[...]) if final else acc


def _ffn(x2, ln2, wup, wdn, fnorm, *, final):
    t = x2.shape[0]
    tm = min(TM_FFN, t)
    row = pl.BlockSpec((tm, D_MODEL), lambda i: (i, 0))
    const = lambda a: pl.BlockSpec(a.shape, lambda i: (0, 0), pipeline_mode=pl.Buffered(1))
    return pl.pallas_call(
        functools.partial(_ffn_kernel, final=final),
        out_shape=jax.ShapeDtypeStruct((t, D_MODEL), F32),
        grid=(t // tm,),
        in_specs=[row] + [const(a) for a in (ln2, wup, wdn, fnorm)],
        out_specs=row,
        compiler_params=pltpu.CompilerParams(dimension_semantics=("arbitrary",), vmem_limit_bytes=VMEM_LIMIT),
        name="ffn",
    )(x2, ln2, wup, wdn, fnorm)


def _head_major(a):
    lead = a.shape[:-1]
    return a.reshape(*lead, 3, B_HEADS, B_DK).swapaxes(-3, -2).reshape(*lead, 3 * D_B)


def _prep_layer(l, ln1, w_in, a_ln_g, a_ln_b, w_s, b_s, conv_w, a_log, dt_bias, o_norm, p_a, p_b, w_o, ln2,
                w_up, w_down):
    w = w_in[l]
    o_qkv = 2 * D_A
    o_z = o_qkv + 3 * D_B
    o_b = o_z + D_B
    o_g = o_b + 2 * B_HEADS
    wba = jnp.zeros((D_MODEL, BA_COLS), F32).at[:, :2 * B_HEADS].set(w[:, o_b:o_g])
    pad_a = lambda v: jnp.zeros((1, BA_COLS), F32).at[0, A_COL0:A_COL0 + B_HEADS].set(v)
    row = lambda v: v.reshape(1, -1)
    mixer_consts = (
        row(ln1[l]), w[:, :o_qkv].astype(BF16), _head_major(w[:, o_qkv:o_z]).astype(BF16),
        w[:, o_z:o_b].astype(BF16),
        wba.astype(BF16), w[:, o_g:].astype(BF16), row(a_ln_g[l]), row(a_ln_b[l]),
        w_s[l], b_s[l].T, _head_major(conv_w[l].T), pad_a(a_log[l]), pad_a(dt_bias[l]), row(o_norm[l]),
        p_a[l].astype(BF16), p_b[l].astype(BF16), w_o[l].astype(BF16))
    ffn_consts = (row(ln2[l]), w_up[l].astype(BF16), w_down[l].astype(BF16))
    return mixer_consts, ffn_consts


def _layer(x, conv0, s0, consts, fnorm, *, valid, final, want_vn):
    b, l, _ = x.shape
    mixer_consts, ffn_consts = consts
    y, conv_state, s_new, vn = _mixer(x, conv0, s0, mixer_consts, valid=valid, want_vn=want_vn)
    out = _ffn(y.reshape(b * l, D_MODEL), *ffn_consts, fnorm, final=final)
    return out.reshape(b, l, D_MODEL), conv_state, s_new, vn


def kernel(x_prompt, x_sample, state_conv, state_delta, ln1, w_in, a_ln_g, a_ln_b, w_s, b_s, conv_w, a_log, dt_bias,
           o_norm, p_a, p_b, w_o, ln2, w_up, w_down, final_norm):
    depth = w_in.shape[0]
    bp, lp, _ = x_prompt.shape
    bs, ls, _ = x_sample.shape
    ls_pad = -(-ls // BLK) * BLK
    yp = x_prompt
    ys = jnp.pad(x_sample, ((0, 0), (0, ls_pad - ls), (0, 0)))
    zero_conv = jnp.zeros((bp, CONV_W - 1, 3 * D_B), F32)
    zero_delta = jnp.zeros((bp, B_HEADS, B_DK, B_DV), F32)
    fnorm = final_norm.reshape(1, -1)
    conv_p, delta_p, conv_s, delta_s, gv_s = [], [], [], [], []
    for l in range(depth):
        consts = _prep_layer(l, ln1, w_in, a_ln_g, a_ln_b, w_s, b_s, conv_w, a_log, dt_bias, o_norm, p_a, p_b, w_o,
                             ln2, w_up, w_down)
        final = l == depth - 1
        yp, cp, dp, _ = _layer(yp, zero_conv, zero_delta, consts, fnorm, valid=min(TS_MIXER, lp), final=final,
                               want_vn=False)
        ys, cs, ds, vs = _layer(ys, state_conv[l], state_delta[l], consts, fnorm, valid=ls, final=final,
                                want_vn=True)
        conv_p.append(cp); delta_p.append(dp); conv_s.append(cs); delta_s.append(ds); gv_s.append(vs[:, :ls])
    return (yp, ys[:, :ls], jnp.stack(conv_p), jnp.stack(delta_p), jnp.stack(conv_s), jnp.stack(delta_s),
            jnp.stack(gv_s))
```

```python
import functools

import jax
import jax.numpy as jnp
from jax import lax
from jax.experimental import pallas as pl
from jax.experimental.pallas import tpu as pltpu

F32 = jnp.float32
BF16 = jnp.bfloat16

D_MODEL = 1024
D_A = 1024
A_GROUPS = 8
A_GROUP_W = D_A // A_GROUPS
B_HEADS = 8
B_DK = 128
B_DV = 128
D_B = B_HEADS * B_DV
CONV_W = 4
D_FF = 4 * D_MODEL
EPS = 1e-6
GMLP_CHUNK = 128

LANES = 128
SUBLANES = 8
BLK = 128
BA_COLS = LANES
A_COL0 = B_HEADS
NEG_BIG = -1e30
VMEM_LIMIT = 60000 * 1024

TM_FFN = 1024
FFN_CHUNK = 1024
TS_MIXER = 256
HEAD_COLS = B_DK + B_DK + B_DV
PROJ_CHUNK = 2 * HEAD_COLS
SLABS = PROJ_CHUNK // LANES


def _rms(x, g):
    return x * lax.rsqrt(jnp.mean(x * x, axis=-1, keepdims=True) + EPS) * g


def _gelu(x):
    half = 0.5 * x
    return half + half * lax.erf(x * 0.7071067811865476)


def _sigmoid(x):
    return jax.nn.sigmoid(x)


def _silu(x):
    return x * _sigmoid(x)


def _dot(a, b):
    return jnp.dot(a, b, preferred_element_type=F32)


def _dot_nt(a, b):
    return lax.dot_general(a, b, (((1,), (1,)), ((), ())), preferred_element_type=F32)


def _split3(v):
    hi = v.astype(BF16)
    r1 = v - hi.astype(F32)
    mid = r1.astype(BF16)
    lo = (r1 - mid.astype(F32)).astype(BF16)
    return hi, mid, lo


def _level_masks():
    row = lax.broadcasted_iota(jnp.int32, (BLK, BLK), 0)
    col = lax.broadcasted_iota(jnp.int32, (BLK, BLK), 1)
    x = row ^ col
    masks = []
    b, k = 1, 0
    while b < BLK:
        masks.append(jnp.where(((x >> k) == 1) & ((row & b) != 0), 1.0, 0.0).astype(F32))
        b, k = 2 * b, k + 1
    return row, col, masks


def _mixer_kernel(x_ref, conv0_ref, s0_ref,
                  ln1_ref, wuv_ref, wqkv_ref, wz_ref, wba_ref, wg_ref, alg_ref, alb_ref,
                  ws_ref, bst_ref, cw_ref, alog_ref, dtb_ref, on_ref, pa_ref, pb_ref, wo_ref,
                  y_ref, convo_ref, so_ref, *rest, ts, valid):
    vn_ref, s_ref, ug_ref, sz_ref, ba_ref, sga_ref, sgb_ref, za_ref, zb_ref, *xpad_refs = rest
    step = pl.program_id(1)
    nblk = ts // BLK
    tail = slice(SUBLANES - (CONV_W - 1), SUBLANES)

    def state_cols(i, s):
        head, part = divmod(i * SLABS + s, HEAD_COLS // LANES)
        c0 = part * D_B + head * B_DK
        return slice(c0, c0 + LANES)

    @pl.when(step == 0)
    def _():
        for i, xp in enumerate(xpad_refs):
            xp[:, 0:SUBLANES, :] = jnp.zeros((SLABS, SUBLANES, LANES), F32)
            for s in range(SLABS):
                xp[s, tail, :] = conv0_ref[:, state_cols(i, s)]
        s_ref[...] = s0_ref[...]

    xn = _rms(x_ref[...], ln1_ref[...]).astype(BF16)
    ba_ref[...] = _dot(xn, wba_ref[...])
    for i, xp in enumerate(xpad_refs):
        sec = _dot(xn, wqkv_ref[:, i * PROJ_CHUNK:(i + 1) * PROJ_CHUNK])
        for s in range(SLABS):
            xp[s, SUBLANES:SUBLANES + ts, :] = sec[:, s * LANES:(s + 1) * LANES]

    def proj_u():
        ug_ref[...] = _gelu(_dot(xn, wuv_ref[:, :D_A]))

    def proj_v():
        va = _gelu(_dot(xn, wuv_ref[:, D_A:]))
        mu = jnp.mean(va, axis=-1, keepdims=True)
        vc = va - mu
        var = jnp.mean(vc * vc, axis=-1, keepdims=True)
        vn_ref[...] = vc * lax.rsqrt(var + EPS) * alg_ref[...] + alb_ref[...]

    def proj_z():
        sz_ref[...] = _silu(_dot(xn, wz_ref[...]))

    def proj_ga():
        sga_ref[...] = _sigmoid(_dot(xn, wg_ref[:, :D_MODEL]))

    def proj_gb():
        sgb_ref[...] = _sigmoid(_dot(xn, wg_ref[:, D_MODEL:]))

    for i, xp in enumerate(xpad_refs):
        for s in range(SLABS):
            convo_ref[:, state_cols(i, s)] = xp[s, SUBLANES + valid - (CONV_W - 1):SUBLANES + valid, :]

    def conv_slab(t0, c0):
        xp, s = xpad_refs[c0 // PROJ_CHUNK], (c0 % PROJ_CHUNK) // LANES
        acc = None
        for j in range(CONV_W):
            r0 = t0 + SUBLANES - (CONV_W - 1) + j
            term = xp[s, r0:r0 + BLK, :] * cw_ref[j:j + 1, c0:c0 + LANES]
            acc = term if acc is None else acc + term
        return _silu(acc)

    row, col, masks = _level_masks()
    causal = row >= col
    strict = row > col
    eye = jnp.where(row == col, 1.0, 0.0).astype(F32)
    tri01 = jnp.where(causal, 1.0, 0.0).astype(BF16)

    def spatial_gating():
        gchunk = min(GMLP_CHUNK, ts)
        for c in range(ts // gchunk):
            r0 = c * gchunk
            for g in range(A_GROUPS):
                c0 = g * A_GROUP_W
                w = jnp.where(causal, ws_ref[g], 0.0).astype(BF16)
                s = _dot(w, vn_ref[r0:r0 + gchunk, c0:c0 + A_GROUP_W].astype(BF16)) + bst_ref[:, g:g + 1]
                za_ref[r0:r0 + gchunk, c0:c0 + A_GROUP_W] = (
                    ug_ref[r0:r0 + gchunk, c0:c0 + A_GROUP_W] * s).astype(BF16)

    pairs = [(c, h) for c in range(nblk) for h in range(B_HEADS)]
    per_blk = []
    for c in range(nblk):
        r0 = c * BLK
        ba = ba_ref[r0:r0 + BLK, :]
        beta_c = _sigmoid(ba)
        g_c = -jnp.exp(alog_ref[...]) * jax.nn.softplus(ba + dtb_ref[...])
        if valid < ts:
            live = lax.broadcasted_iota(jnp.int32, (BLK, BA_COLS), 0) + r0 < valid
            beta_c = jnp.where(live, beta_c, 0.0)
            g_c = jnp.where(live, g_c, 0.0)
        gam_c = sum(_dot(tri01, p) for p in _split3(g_c))
        per_blk.append((beta_c, gam_c, gam_c.T))

    def prepare(c, h):
        r0 = c * BLK
        beta_c, gam_c, gam_t = per_blk[c]
        q = conv_slab(r0, h * HEAD_COLS)
        k = conv_slab(r0, h * HEAD_COLS + B_DK)
        v = conv_slab(r0, h * HEAD_COLS + 2 * B_DK)
        q = q * lax.rsqrt(jnp.sum(q * q, axis=-1, keepdims=True) + EPS) * (B_DK ** -0.5)
        k = k * lax.rsqrt(jnp.sum(k * k, axis=-1, keepdims=True) + EPS)
        beta = beta_c[:, h:h + 1]
        gam = jnp.broadcast_to(gam_c[:, A_COL0 + h:A_COL0 + h + 1], (BLK, LANES))
        gam_row = gam_t[A_COL0 + h:A_COL0 + h + 1, :]
        gam_last = gam[BLK - 1:BLK, :]
        eg = jnp.exp(gam)
        kb = k * beta
        kq = _dot_nt(jnp.concatenate([kb.astype(BF16), q.astype(BF16)], axis=0), k.astype(BF16))
        decay = jnp.exp(jnp.where(causal, gam - gam_row, NEG_BIG))
        kgt = (k * jnp.exp(gam_last - gam)).T.astype(BF16)
        return dict(
            g=jnp.where(strict, -(kq[:BLK] * decay), 0.0),
            pk=jnp.concatenate([(kq[BLK:] * decay).astype(BF16), kgt], axis=0),
            vb=(v * beta).astype(BF16), kbg=(kb * eg).astype(BF16), qg=(q * eg).astype(BF16),
            gl=jnp.exp(gam_last))

    dense_stages = ((proj_u,), (proj_v,), (proj_z,), (proj_ga, proj_gb))
    heads_per_group = B_HEADS // len(dense_stages)
    st = {}
    for gi, stages in enumerate(dense_stages):
        for stage in stages:
            stage()
        for c in range(nblk):
            for h in range(gi * heads_per_group, (gi + 1) * heads_per_group):
                st[c, h] = prepare(c, h)
    spatial_gating()
    mix_a = sga_ref[...] * _dot(za_ref[...], pa_ref[...])

    for pr in pairs:
        d = st[pr]
        g = d["g"]
        subdiag = jnp.sum(g * masks[0], axis=0, keepdims=True)
        d["g"] = g + pltpu.roll(g, LANES - 1, axis=1) * subdiag
    for m in masks[1:]:
        for pr in pairs:
            d = st[pr]
            g = d["g"]
            d["g"] = g + _dot(g.astype(BF16), (g * m).astype(BF16))
    for pr in pairs:
        d = st[pr]
        tinv = (eye + d["g"]).astype(BF16)
        d["u"] = _dot(tinv, d["vb"])
        d["wq"] = jnp.concatenate([_dot(tinv, d["kbg"]).astype(BF16), d["qg"]], axis=0)

    def merge(c):
        rows = slice(c * BLK, (c + 1) * BLK)
        mix = mix_a[rows] + sgb_ref[rows, :] * _dot(zb_ref[rows, :], pb_ref[...])
        y_ref[rows, :] = x_ref[rows, :] + _dot(mix.astype(BF16), wo_ref[...])

    for c in range(nblk):
        r0 = c * BLK
        cur = {}
        for h in range(B_HEADS):
            s_old = s_ref[h]
            wqs = _dot(st[c, h]["wq"], s_old.astype(BF16))
            cur[h] = (s_old, wqs[:BLK], wqs[BLK:])
        if c > 0:
            merge(c - 1)
        for h in range(B_HEADS):
            d = st[c, h]
            s_old, ws, qs = cur[h]
            pkv = _dot(d["pk"], (d["u"] - ws).astype(BF16))
            cur[h] = (s_old, qs + pkv[:BLK], pkv[BLK:])
        for h in range(B_HEADS):
            d = st[c, h]
            s_old, o, ds = cur[h]
            s_ref[h] = s_old * d["gl"] + ds
            o = o * lax.rsqrt(jnp.mean(o * o, axis=-1, keepdims=True) + EPS) * on_ref[...]
            hc = h * B_DV
            zb_ref[r0:r0 + BLK, hc:hc + B_DV] = (o * sz_ref[r0:r0 + BLK, hc:hc + B_DV]).astype(BF16)

    merge(nblk - 1)

    for xp in xpad_refs:
        xp[:, 0:SUBLANES, :] = xp[:, ts:ts + SUBLANES, :]

    @pl.when(step == pl.num_programs(1) - 1)
    def _():
        so_ref[...] = s_ref[...]


def _mixer(x, conv0, s0, consts, *, valid, want_vn):
    b, l, _ = x.shape
    ts = min(TS_MIXER, l)
    assert l % ts == 0 and ts % BLK == 0
    assert valid == ts or l == ts
    tok = lambda c: pl.BlockSpec((None, ts, c), lambda i, j: (i, j, 0))
    const = lambda a: pl.BlockSpec(a.shape, lambda i, j: (0,) * a.ndim, pipeline_mode=pl.Buffered(1))
    per_seq = lambda a: pl.BlockSpec((None,) + a.shape[1:], lambda i, j: (i,) + (0,) * (a.ndim - 1))
    tile = lambda c, dt: pltpu.VMEM((ts, c), dt)
    out_shape = [jax.ShapeDtypeStruct((b, l, D_MODEL), F32),
                 jax.ShapeDtypeStruct((b, CONV_W - 1, 3 * D_B), F32),
                 jax.ShapeDtypeStruct((b, B_HEADS, B_DK, B_DV), F32)]
    out_specs = [tok(D_MODEL),
                 pl.BlockSpec((None, CONV_W - 1, 3 * D_B), lambda i, j: (i, 0, 0)),
                 pl.BlockSpec((None, B_HEADS, B_DK, B_DV), lambda i, j: (i, 0, 0, 0))]
    scratch = [pltpu.VMEM((B_HEADS, B_DK, B_DV), F32),
               tile(D_A, F32), tile(D_B, F32), tile(BA_COLS, F32), tile(D_MODEL, F32), tile(D_MODEL, F32),
               tile(D_A, BF16), tile(D_B, BF16)]
    scratch += [pltpu.VMEM((SLABS, ts + SUBLANES, LANES), F32)] * (3 * D_B // PROJ_CHUNK)
    if want_vn:
        out_shape.append(jax.ShapeDtypeStruct((b, l, D_A), F32))
        out_specs.append(tok(D_A))
    else:
        scratch.insert(0, tile(D_A, F32))
    outs = pl.pallas_call(
        functools.partial(_mixer_kernel, ts=ts, valid=valid),
        out_shape=out_shape,
        grid=(b, l // ts),
        in_specs=[tok(D_MODEL), per_seq(conv0), per_seq(s0)] + [const(a) for a in consts],
        out_specs=out_specs,
        scratch_shapes=scratch,
        compiler_params=pltpu.CompilerParams(dimension_semantics=("arbitrary", "arbitrary"),
                                             vmem_limit_bytes=VMEM_LIMIT),
        name="mixer",
    )(x, conv0, s0, *consts)
    return tuple(outs) if want_vn else tuple(outs) + (None,)


def _ffn_kernel(x_ref, ln2_ref, wup_ref, wdn_ref, fn_ref, o_ref, *, final):
    x = x_ref[...]
    h = _rms(x, ln2_ref[...]).astype(BF16)
    acc = x
    for c in range(D_FF // FFN_CHUNK):
        c0 = c * FFN_CHUNK
        hid = _dot(h, wup_ref[:, c0:c0 + FFN_CHUNK])
        act = jnp.square(jnp.maximum(hid, 0.0)).astype(BF16)
        acc = acc + _dot(act, wdn_ref[c0:c0 + FFN_CHUNK, :])
    o_ref[...] = _rms(acc, fn_ref[...]) if final else acc


def _ffn(x2, ln2, wup, wdn, fnorm, *, final):
    t = x2.shape[0]
    tm = min(TM_FFN, t)
    row = pl.BlockSpec((tm, D_MODEL), lambda i: (i, 0))
    const = lambda a: pl.BlockSpec(a.shape, lambda i: (0, 0), pipeline_mode=pl.Buffered(1))
    return pl.pallas_call(
        functools.partial(_ffn_kernel, final=final),
        out_shape=jax.ShapeDtypeStruct((t, D_MODEL), F32),
        grid=(t // tm,),
        in_specs=[row] + [const(a) for a in (ln2, wup, wdn, fnorm)],
        out_specs=row,
        compiler_params=pltpu.CompilerParams(dimension_semantics=("arbitrary",), vmem_limit_bytes=VMEM_LIMIT),
        name="ffn",
    )(x2, ln2, wup, wdn, fnorm)


def _head_major(a):
    lead = a.shape[:-1]
    return a.reshape(*lead, 3, B_HEADS, B_DK).swapaxes(-3, -2).reshape(*lead, 3 * D_B)


def _prep_layer(l, ln1, w_in, a_ln_g, a_ln_b, w_s, b_s, conv_w, a_log, dt_bias, o_norm, p_a, p_b, w_o, ln2,
                w_up, w_down):
    w = w_in[l]
    o_qkv = 2 * D_A
    o_z = o_qkv + 3 * D_B
    o_b = o_z + D_B
    o_g = o_b + 2 * B_HEADS
    wba = jnp.zeros((D_MODEL, BA_COLS), F32).at[:, :2 * B_HEADS].set(w[:, o_b:o_g])
    pad_a = lambda v: jnp.zeros((1, BA_COLS), F32).at[0, A_COL0:A_COL0 + B_HEADS].set(v)
    row = lambda v: v.reshape(1, -1)
    mixer_consts = (
        row(ln1[l]), w[:, :o_qkv].astype(BF16), _head_major(w[:, o_qkv:o_z]).astype(BF16),
        w[:, o_z:o_b].astype(BF16),
        wba.astype(BF16), w[:, o_g:].astype(BF16), row(a_ln_g[l]), row(a_ln_b[l]),
        w_s[l], b_s[l].T, _head_major(conv_w[l].T), pad_a(a_log[l]), pad_a(dt_bias[l]), row(o_norm[l]),
        p_a[l].astype(BF16), p_b[l].astype(BF16), w_o[l].astype(BF16))
    ffn_consts = (row(ln2[l]), w_up[l].astype(BF16), w_down[l].astype(BF16))
    return mixer_consts, ffn_consts


def _layer(x, conv0, s0, consts, fnorm, *, final, want_vn):
    b, lv, _ = x.shape
    l = -(-lv // BLK) * BLK
    mixer_consts, ffn_consts = consts
    if l == lv:
        y, conv_state, s_new, vn = _mixer(x, conv0, s0, mixer_consts, valid=min(TS_MIXER, l), want_vn=want_vn)
    else:
        xp = jnp.pad(x, ((0, 0), (0, l - lv), (0, 0)))
        y, conv_state, s_new, vn = _mixer(xp, conv0, s0, mixer_consts, valid=lv, want_vn=want_vn)
        y = y[:, :lv]
        vn = vn[:, :lv] if want_vn else None
    out = _ffn(y.reshape(b * lv, D_MODEL), *ffn_consts, fnorm, final=final)
    return out.reshape(b, lv, D_MODEL), conv_state, s_new, vn


def kernel(x_prompt, x_sample, state_conv, state_delta, ln1, w_in, a_ln_g, a_ln_b, w_s, b_s, conv_w, a_log, dt_bias,
           o_norm, p_a, p_b, w_o, ln2, w_up, w_down, final_norm):
    depth = w_in.shape[0]
    bp = x_prompt.shape[0]
    yp, ys = x_prompt, x_sample
    zero_conv = jnp.zeros((bp, CONV_W - 1, 3 * D_B), F32)
    zero_delta = jnp.zeros((bp, B_HEADS, B_DK, B_DV), F32)
    fnorm = final_norm.reshape(1, -1)
    conv_p, delta_p, conv_s, delta_s, gv_s = [], [], [], [], []
    for l in range(depth):
        consts = _prep_layer(l, ln1, w_in, a_ln_g, a_ln_b, w_s, b_s, conv_w, a_log, dt_bias, o_norm, p_a, p_b, w_o,
                             ln2, w_up, w_down)
        final = l == depth - 1
        yp, cp, dp, _ = _layer(yp, zero_conv, zero_delta, consts, fnorm, final=final, want_vn=False)
        ys, cs, ds, vs = _layer(ys, state_conv[l], state_delta[l], consts, fnorm, final=final, want_vn=True)
        conv_p.append(cp); delta_p.append(dp); conv_s.append(cs); delta_s.append(ds); gv_s.append(vs)
    return (yp, ys, jnp.stack(conv_p), jnp.stack(delta_p), jnp.stack(conv_s), jnp.stack(delta_s), jnp.stack(gv_s))
```

```python
import functools

import jax
import jax.numpy as jnp
from jax import lax
from jax.experimental import pallas as pl
from jax.experimental.pallas import tpu as pltpu

F32 = jnp.float32
BF16 = jnp.bfloat16

D_MODEL = 1024
D_A = 1024
A_GROUPS = 8
A_GROUP_W = D_A // A_GROUPS
B_HEADS = 8
B_DK = 128
B_DV = 128
D_B = B_HEADS * B_DV
CONV_W = 4
D_FF = 4 * D_MODEL
EPS = 1e-6
GMLP_CHUNK = 128

LANES = 128
SUBLANES = 8
BLK = 128
BA_COLS = LANES
A_COL0 = B_HEADS
NEG_BIG = -1e30
VMEM_LIMIT = 60000 * 1024

TM_FFN = 1024
FFN_CHUNK = 1024
TS_MIXER = 256
HEAD_COLS = B_DK + B_DK + B_DV
PROJ_CHUNK = 2 * HEAD_COLS
SLABS = PROJ_CHUNK // LANES


def _rms(x, g):
    return x * lax.rsqrt(jnp.mean(x * x, axis=-1, keepdims=True) + EPS) * g


def _gelu(x):
    half = 0.5 * x
    return half + half * lax.erf(x * 0.7071067811865476)


def _sigmoid(x):
    return jax.nn.sigmoid(x)


def _silu(x):
    return x * _sigmoid(x)


def _dot(a, b):
    return jnp.dot(a, b, preferred_element_type=F32)


def _dot_nt(a, b):
    return lax.dot_general(a, b, (((1,), (1,)), ((), ())), preferred_element_type=F32)


def _split3(v):
    hi = v.astype(BF16)
    r1 = v - hi.astype(F32)
    mid = r1.astype(BF16)
    lo = (r1 - mid.astype(F32)).astype(BF16)
    return hi, mid, lo


def _level_masks():
    row = lax.broadcasted_iota(jnp.int32, (BLK, BLK), 0)
    col = lax.broadcasted_iota(jnp.int32, (BLK, BLK), 1)
    x = row ^ col
    masks = []
    b, k = 1, 0
    while b < BLK:
        masks.append(jnp.where(((x >> k) == 1) & ((row & b) != 0), 1.0, 0.0).astype(F32))
        b, k = 2 * b, k + 1
    return row, col, masks


def _mixer_kernel(x_ref, conv0_ref, s0_ref,
                  ln1_ref, wuv_ref, wqkv_ref, wz_ref, wba_ref, wg_ref, alg_ref, alb_ref,
                  ws_ref, bst_ref, cw_ref, alog_ref, dtb_ref, on_ref, pa_ref, pb_ref, wo_ref,
                  y_ref, convo_ref, so_ref, *rest, ts, valid):
    vn_ref, s_ref, ug_ref, sz_ref, ba_ref, sga_ref, sgb_ref, za_ref, zb_ref, *xpad_refs = rest
    step = pl.program_id(1)
    nblk = ts // BLK
    tail = slice(SUBLANES - (CONV_W - 1), SUBLANES)

    def state_cols(i, s):
        head, part = divmod(i * SLABS + s, HEAD_COLS // LANES)
        c0 = part * D_B + head * B_DK
        return slice(c0, c0 + LANES)

    @pl.when(step == 0)
    def _():
        for i, xp in enumerate(xpad_refs):
            xp[:, 0:SUBLANES, :] = jnp.zeros((SLABS, SUBLANES, LANES), F32)
            for s in range(SLABS):
                xp[s, tail, :] = conv0_ref[:, state_cols(i, s)]
        s_ref[...] = s0_ref[...]

    xn = _rms(x_ref[...], ln1_ref[...]).astype(BF16)
    ba_ref[...] = _dot(xn, wba_ref[...])
    for i, xp in enumerate(xpad_refs):
        sec = _dot(xn, wqkv_ref[:, i * PROJ_CHUNK:(i + 1) * PROJ_CHUNK])
        for s in range(SLABS):
            xp[s, SUBLANES:SUBLANES + ts, :] = sec[:, s * LANES:(s + 1) * LANES]

    def proj_u():
        ug_ref[...] = _gelu(_dot(xn, wuv_ref[:, :D_A]))

    def proj_v():
        va = _gelu(_dot(xn, wuv_ref[:, D_A:]))
        mu = jnp.mean(va, axis=-1, keepdims=True)
        vc = va - mu
        var = jnp.mean(vc * vc, axis=-1, keepdims=True)
        vn_ref[...] = vc * lax.rsqrt(var + EPS) * alg_ref[...] + alb_ref[...]

    def proj_z():
        sz_ref[...] = _silu(_dot(xn, wz_ref[...]))

    def proj_ga():
        sga_ref[...] = _sigmoid(_dot(xn, wg_ref[:, :D_MODEL]))

    def proj_gb():
        sgb_ref[...] = _sigmoid(_dot(xn, wg_ref[:, D_MODEL:]))

    for i, xp in enumerate(xpad_refs):
        for s in range(SLABS):
            convo_ref[:, state_cols(i, s)] = xp[s, SUBLANES + valid - (CONV_W - 1):SUBLANES + valid, :]

    def conv_slab(t0, c0):
        xp, s = xpad_refs[c0 // PROJ_CHUNK], (c0 % PROJ_CHUNK) // LANES
        acc = None
        for j in range(CONV_W):
            r0 = t0 + SUBLANES - (CONV_W - 1) + j
            term = xp[s, r0:r0 + BLK, :] * cw_ref[j:j + 1, c0:c0 + LANES]
            acc = term if acc is None else acc + term
        return _silu(acc)

    row, col, masks = _level_masks()
    causal = row >= col
    strict = row > col
    eye = jnp.where(row == col, 1.0, 0.0).astype(F32)
    tri01 = jnp.where(causal, 1.0, 0.0).astype(BF16)

    def spatial_gating():
        gchunk = min(GMLP_CHUNK, ts)
        for c in range(ts // gchunk):
            r0 = c * gchunk
            for g in range(A_GROUPS):
                c0 = g * A_GROUP_W
                w = jnp.where(causal, ws_ref[g], 0.0).astype(BF16)
                s = _dot(w, vn_ref[r0:r0 + gchunk, c0:c0 + A_GROUP_W].astype(BF16)) + bst_ref[:, g:g + 1]
                za_ref[r0:r0 + gchunk, c0:c0 + A_GROUP_W] = (
                    ug_ref[r0:r0 + gchunk, c0:c0 + A_GROUP_W] * s).astype(BF16)

    pairs = [(c, h) for c in range(nblk) for h in range(B_HEADS)]
    per_blk = []
    for c in range(nblk):
        r0 = c * BLK
        ba = ba_ref[r0:r0 + BLK, :]
        beta_c = _sigmoid(ba)
        g_c = -jnp.exp(alog_ref[...]) * jax.nn.softplus(ba + dtb_ref[...])
        if valid < ts:
            live = lax.broadcasted_iota(jnp.int32, (BLK, BA_COLS), 0) + r0 < valid
            beta_c = jnp.where(live, beta_c, 0.0)
            g_c = jnp.where(live, g_c, 0.0)
        gam_c = sum(_dot(tri01, p) for p in _split3(g_c))
        per_blk.append((beta_c, gam_c, gam_c.T))

    def prepare(c, h):
        r0 = c * BLK
        beta_c, gam_c, gam_t = per_blk[c]
        q = conv_slab(r0, h * HEAD_COLS)
        k = conv_slab(r0, h * HEAD_COLS + B_DK)
        v = conv_slab(r0, h * HEAD_COLS + 2 * B_DK)
        q = q * lax.rsqrt(jnp.sum(q * q, axis=-1, keepdims=True) + EPS) * (B_DK ** -0.5)
        k = k * lax.rsqrt(jnp.sum(k * k, axis=-1, keepdims=True) + EPS)
        beta = beta_c[:, h:h + 1]
        gam = jnp.broadcast_to(gam_c[:, A_COL0 + h:A_COL0 + h + 1], (BLK, LANES))
        gam_row = gam_t[A_COL0 + h:A_COL0 + h + 1, :]
        gam_last = gam[BLK - 1:BLK, :]
        eg = jnp.exp(gam)
        kb = k * beta
        kq = _dot_nt(jnp.concatenate([kb.astype(BF16), q.astype(BF16)], axis=0), k.astype(BF16))
        decay = jnp.exp(jnp.where(causal, gam - gam_row, NEG_BIG))
        kgt = (k * jnp.exp(gam_last - gam)).T.astype(BF16)
        return dict(
            g=jnp.where(strict, -(kq[:BLK] * decay), 0.0),
            pk=jnp.concatenate([(kq[BLK:] * decay).astype(BF16), kgt], axis=0),
            vb=(v * beta).astype(BF16), kbg=(kb * eg).astype(BF16), qg=(q * eg).astype(BF16),
            gl=jnp.exp(gam_last))

    dense_stages = ((proj_u,), (proj_v,), (proj_z,), (proj_ga, proj_gb))
    heads_per_group = B_HEADS // len(dense_stages)
    st = {}
    for gi, stages in enumerate(dense_stages):
        for stage in stages:
            stage()
        for c in range(nblk):
            for h in range(gi * heads_per_group, (gi + 1) * heads_per_group):
                st[c, h] = prepare(c, h)
    spatial_gating()
    mix_a = sga_ref[...] * _dot(za_ref[...], pa_ref[...])

    for pr in pairs:
        d = st[pr]
        g = d["g"]
        subdiag = jnp.sum(g * masks[0], axis=0, keepdims=True)
        d["g"] = g + pltpu.roll(g, LANES - 1, axis=1) * subdiag
    for k, m in enumerate(masks[1:], start=1):
        for c, h in pairs:
            if 2 ** k >= min(valid - c * BLK, BLK):
                continue
            d = st[c, h]
            g = d["g"]
            d["g"] = g + _dot(g.astype(BF16), (g * m).astype(BF16))
    for pr in pairs:
        d = st[pr]
        tinv = (eye + d["g"]).astype(BF16)
        d["u"] = _dot(tinv, d["vb"])
        d["wq"] = jnp.concatenate([_dot(tinv, d["kbg"]).astype(BF16), d["qg"]], axis=0)

    def merge(c):
        rows = slice(c * BLK, (c + 1) * BLK)
        mix = mix_a[rows] + sgb_ref[rows, :] * _dot(zb_ref[rows, :], pb_ref[...])
        y_ref[rows, :] = x_ref[rows, :] + _dot(mix.astype(BF16), wo_ref[...])

    for c in range(nblk):
        r0 = c * BLK
        cur = {}
        for h in range(B_HEADS):
            s_old = s_ref[h]
            wqs = _dot(st[c, h]["wq"], s_old.astype(BF16))
            cur[h] = (s_old, wqs[:BLK], wqs[BLK:])
        if c > 0:
            merge(c - 1)
        for h in range(B_HEADS):
            d = st[c, h]
            s_old, ws, qs = cur[h]
            pkv = _dot(d["pk"], (d["u"] - ws).astype(BF16))
            cur[h] = (s_old, qs + pkv[:BLK], pkv[BLK:])
        for h in range(B_HEADS):
            d = st[c, h]
            s_old, o, ds = cur[h]
            s_ref[h] = s_old * d["gl"] + ds
            o = o * lax.rsqrt(jnp.mean(o * o, axis=-1, keepdims=True) + EPS) * on_ref[...]
            hc = h * B_DV
            zb_ref[r0:r0 + BLK, hc:hc + B_DV] = (o * sz_ref[r0:r0 + BLK, hc:hc + B_DV]).astype(BF16)

    merge(nblk - 1)

    for xp in xpad_refs:
        xp[:, 0:SUBLANES, :] = xp[:, ts:ts + SUBLANES, :]

    @pl.when(step == pl.num_programs(1) - 1)
    def _():
        so_ref[...] = s_ref[...]


def _mixer(x, conv0, s0, consts, *, valid, want_vn):
    b, l, _ = x.shape
    ts = min(TS_MIXER, l)
    assert l % ts == 0 and ts % BLK == 0
    assert valid == ts or l == ts
    tok = lambda c: pl.BlockSpec((None, ts, c), lambda i, j: (i, j, 0))
    const = lambda a: pl.BlockSpec(a.shape, lambda i, j: (0,) * a.ndim, pipeline_mode=pl.Buffered(1))
    per_seq = lambda a: pl.BlockSpec((None,) + a.shape[1:], lambda i, j: (i,) + (0,) * (a.ndim - 1))
    tile = lambda c, dt: pltpu.VMEM((ts, c), dt)
    out_shape = [jax.ShapeDtypeStruct((b, l, D_MODEL), F32),
                 jax.ShapeDtypeStruct((b, CONV_W - 1, 3 * D_B), F32),
                 jax.ShapeDtypeStruct((b, B_HEADS, B_DK, B_DV), F32)]
    out_specs = [tok(D_MODEL),
                 pl.BlockSpec((None, CONV_W - 1, 3 * D_B), lambda i, j: (i, 0, 0)),
                 pl.BlockSpec((None, B_HEADS, B_DK, B_DV), lambda i, j: (i, 0, 0, 0))]
    scratch = [pltpu.VMEM((B_HEADS, B_DK, B_DV), F32),
               tile(D_A, F32), tile(D_B, F32), tile(BA_COLS, F32), tile(D_MODEL, F32), tile(D_MODEL, F32),
               tile(D_A, BF16), tile(D_B, BF16)]
    scratch += [pltpu.VMEM((SLABS, ts + SUBLANES, LANES), F32)] * (3 * D_B // PROJ_CHUNK)
    if want_vn:
        out_shape.append(jax.ShapeDtypeStruct((b, l, D_A), F32))
        out_specs.append(tok(D_A))
    else:
        scratch.insert(0, tile(D_A, F32))
    outs = pl.pallas_call(
        functools.partial(_mixer_kernel, ts=ts, valid=valid),
        out_shape=out_shape,
        grid=(b, l // ts),
        in_specs=[tok(D_MODEL), per_seq(conv0), per_seq(s0)] + [const(a) for a in consts],
        out_specs=out_specs,
        scratch_shapes=scratch,
        compiler_params=pltpu.CompilerParams(dimension_semantics=("arbitrary", "arbitrary"),
                                             vmem_limit_bytes=VMEM_LIMIT),
        name="mixer",
    )(x, conv0, s0, *consts)
    return tuple(outs) if want_vn else tuple(outs) + (None,)


def _ffn_kernel(x_ref, ln2_ref, wup_ref, wdn_ref, fn_ref, o_ref, *, final):
    x = x_ref[...]
    h = _rms(x, ln2_ref[...]).astype(BF16)
    acc = x
    for c in range(D_FF // FFN_CHUNK):
        c0 = c * FFN_CHUNK
        hid = _dot(h, wup_ref[:, c0:c0 + FFN_CHUNK])
        act = jnp.square(jnp.maximum(hid, 0.0)).astype(BF16)
        acc = acc + _dot(act, wdn_ref[c0:c0 + FFN_CHUNK, :])
    o_ref[...] = _rms(acc, fn_ref[...]) if final else acc


def _ffn(x2, ln2, wup, wdn, fnorm, *, final):
    t = x2.shape[0]
    tm = min(TM_FFN, t)
    row = pl.BlockSpec((tm, D_MODEL), lambda i: (i, 0))
    const = lambda a: pl.BlockSpec(a.shape, lambda i: (0, 0), pipeline_mode=pl.Buffered(1))
    return pl.pallas_call(
        functools.partial(_ffn_kernel, final=final),
        out_shape=jax.ShapeDtypeStruct((t, D_MODEL), F32),
        grid=(t // tm,),
        in_specs=[row] + [const(a) for a in (ln2, wup, wdn, fnorm)],
        out_specs=row,
        compiler_params=pltpu.CompilerParams(dimension_semantics=("arbitrary",), vmem_limit_bytes=VMEM_LIMIT),
        name="ffn",
    )(x2, ln2, wup, wdn, fnorm)


def _head_major(a):
    lead = a.shape[:-1]
    return a.reshape(*lead, 3, B_HEADS, B_DK).swapaxes(-3, -2).reshape(*lead, 3 * D_B)


def _prep_layer(l, ln1, w_in, a_ln_g, a_ln_b, w_s, b_s, conv_w, a_log, dt_bias, o_norm, p_a, p_b, w_o, ln2,
                w_up, w_down):
    w = w_in[l]
    o_qkv = 2 * D_A
    o_z = o_qkv + 3 * D_B
    o_b = o_z + D_B
    o_g = o_b + 2 * B_HEADS
    wba = jnp.zeros((D_MODEL, BA_COLS), F32).at[:, :2 * B_HEADS].set(w[:, o_b:o_g])
    pad_a = lambda v: jnp.zeros((1, BA_COLS), F32).at[0, A_COL0:A_COL0 + B_HEADS].set(v)
    row = lambda v: v.reshape(1, -1)
    mixer_consts = (
        row(ln1[l]), w[:, :o_qkv].astype(BF16), _head_major(w[:, o_qkv:o_z].astype(BF16)),
        w[:, o_z:o_b].astype(BF16),
        wba.astype(BF16), w[:, o_g:].astype(BF16), row(a_ln_g[l]), row(a_ln_b[l]),
        w_s[l], b_s[l].T, _head_major(conv_w[l].T), pad_a(a_log[l]), pad_a(dt_bias[l]), row(o_norm[l]),
        p_a[l].astype(BF16), p_b[l].astype(BF16), w_o[l].astype(BF16))
    ffn_consts = (row(ln2[l]), w_up[l].astype(BF16), w_down[l].astype(BF16))
    return mixer_consts, ffn_consts


def _layer(x, conv0, s0, consts, fnorm, *, final, want_vn):
    b, lv, _ = x.shape
    l = -(-lv // BLK) * BLK
    mixer_consts, ffn_consts = consts
    if l == lv:
        y, conv_state, s_new, vn = _mixer(x, conv0, s0, mixer_consts, valid=min(TS_MIXER, l), want_vn=want_vn)
    else:
        xp = jnp.pad(x, ((0, 0), (0, l - lv), (0, 0)))
        y, conv_state, s_new, vn = _mixer(xp, conv0, s0, mixer_consts, valid=lv, want_vn=want_vn)
        y = y[:, :lv]
        vn = vn[:, :lv] if want_vn else None
    out = _ffn(y.reshape(b * lv, D_MODEL), *ffn_consts, fnorm, final=final)
    return out.reshape(b, lv, D_MODEL), conv_state, s_new, vn


def kernel(x_prompt, x_sample, state_conv, state_delta, ln1, w_in, a_ln_g, a_ln_b, w_s, b_s, conv_w, a_log, dt_bias,
           o_norm, p_a, p_b, w_o, ln2, w_up, w_down, final_norm):
    depth = w_in.shape[0]
    bp = x_prompt.shape[0]
    yp, ys = x_prompt, x_sample
    zero_conv = jnp.zeros((bp, CONV_W - 1, 3 * D_B), F32)
    zero_delta = jnp.zeros((bp, B_HEADS, B_DK, B_DV), F32)
    fnorm = final_norm.reshape(1, -1)
    conv_p, delta_p, conv_s, delta_s, gv_s = [], [], [], [], []
    for l in range(depth):
        consts = _prep_layer(l, ln1, w_in, a_ln_g, a_ln_b, w_s, b_s, conv_w, a_log, dt_bias, o_norm, p_a, p_b, w_o,
                             ln2, w_up, w_down)
        final = l == depth - 1
        yp, cp, dp, _ = _layer(yp, zero_conv, zero_delta, consts, fnorm, final=final, want_vn=False)
        ys, cs, ds, vs = _layer(ys, state_conv[l], state_delta[l], consts, fnorm, final=final, want_vn=True)
        conv_p.append(cp); delta_p.append(dp); conv_s.append(cs); delta_s.append(ds); gv_s.append(vs)
    return (yp, ys, jnp.stack(conv_p), jnp.stack(delta_p), jnp.stack(conv_s), jnp.stack(delta_s), jnp.stack(gv_s))
```

```python
import functools

import jax
import jax.numpy as jnp
from jax import lax
from jax.experimental import pallas as pl
from jax.experimental.pallas import tpu as pltpu

F32 = jnp.float32
BF16 = jnp.bfloat16

D_MODEL = 1024
D_A = 1024
A_GROUPS = 8
A_GROUP_W = D_A // A_GROUPS
B_HEADS = 8
B_DK = 128
B_DV = 128
D_B = B_HEADS * B_DV
CONV_W = 4
D_FF = 4 * D_MODEL
EPS = 1e-6
GMLP_CHUNK = 128

LANES = 128
SUBLANES = 8
BLK = 128
BA_COLS = LANES
A_COL0 = B_HEADS
NEG_BIG = -1e30
VMEM_LIMIT = 60000 * 1024

TM_FFN = 1024
FFN_CHUNK = 1024
TS_MIXER = 256
HEAD_COLS = B_DK + B_DK + B_DV
PROJ_CHUNK = 2 * HEAD_COLS
SLABS = PROJ_CHUNK // LANES


def _rms(x, g):
    return x * lax.rsqrt(jnp.mean(x * x, axis=-1, keepdims=True) + EPS) * g


def _gelu(x):
    half = 0.5 * x
    return half + half * lax.erf(x * 0.7071067811865476)


def _sigmoid(x):
    return jax.nn.sigmoid(x)


def _silu(x):
    return x * _sigmoid(x)


def _dot(a, b):
    return jnp.dot(a, b, preferred_element_type=F32)


def _dot_nt(a, b):
    return lax.dot_general(a, b, (((1,), (1,)), ((), ())), preferred_element_type=F32)


def _split3(v):
    hi = v.astype(BF16)
    r1 = v - hi.astype(F32)
    mid = r1.astype(BF16)
    lo = (r1 - mid.astype(F32)).astype(BF16)
    return hi, mid, lo


def _level_masks():
    row = lax.broadcasted_iota(jnp.int32, (BLK, BLK), 0)
    col = lax.broadcasted_iota(jnp.int32, (BLK, BLK), 1)
    x = row ^ col
    masks = []
    b, k = 1, 0
    while b < BLK:
        masks.append(jnp.where(((x >> k) == 1) & ((row & b) != 0), 1.0, 0.0).astype(F32))
        b, k = 2 * b, k + 1
    return row, col, masks


def _mixer_kernel(x_ref, conv0_ref, s0_ref,
                  ln1_ref, wuv_ref, wqkv_ref, wz_ref, wba_ref, wg_ref, alg_ref, alb_ref,
                  ws_ref, bst_ref, cw_ref, alog_ref, dtb_ref, on_ref, pa_ref, pb_ref, wo_ref,
                  y_ref, convo_ref, so_ref, *rest, ts, valid):
    vn_ref, s_ref, ug_ref, sz_ref, ba_ref, sga_ref, sgb_ref, za_ref, zb_ref, *xpad_refs = rest
    step = pl.program_id(1)
    nblk = ts // BLK
    tail = slice(SUBLANES - (CONV_W - 1), SUBLANES)

    def state_cols(i, s):
        head, part = divmod(i * SLABS + s, HEAD_COLS // LANES)
        c0 = part * D_B + head * B_DK
        return slice(c0, c0 + LANES)

    @pl.when(step == 0)
    def _():
        for i, xp in enumerate(xpad_refs):
            xp[:, 0:SUBLANES, :] = jnp.zeros((SLABS, SUBLANES, LANES), F32)
            for s in range(SLABS):
                xp[s, tail, :] = conv0_ref[:, state_cols(i, s)]
        s_ref[...] = s0_ref[...]

    xn = _rms(x_ref[...], ln1_ref[...]).astype(BF16)
    ba_ref[...] = _dot(xn, wba_ref[...])
    for i, xp in enumerate(xpad_refs):
        sec = _dot(xn, wqkv_ref[:, i * PROJ_CHUNK:(i + 1) * PROJ_CHUNK])
        for s in range(SLABS):
            xp[s, SUBLANES:SUBLANES + ts, :] = sec[:, s * LANES:(s + 1) * LANES]

    def proj_u():
        ug_ref[...] = _gelu(_dot(xn, wuv_ref[:, :D_A]))

    def proj_v():
        va = _gelu(_dot(xn, wuv_ref[:, D_A:]))
        mu = jnp.mean(va, axis=-1, keepdims=True)
        vc = va - mu
        var = jnp.mean(vc * vc, axis=-1, keepdims=True)
        vn_ref[...] = vc * lax.rsqrt(var + EPS) * alg_ref[...] + alb_ref[...]

    def proj_z():
        sz_ref[...] = _silu(_dot(xn, wz_ref[...]))

    def proj_ga():
        sga_ref[...] = _sigmoid(_dot(xn, wg_ref[:, :D_MODEL]))

    def proj_gb():
        sgb_ref[...] = _sigmoid(_dot(xn, wg_ref[:, D_MODEL:]))

    for i, xp in enumerate(xpad_refs):
        for s in range(SLABS):
            convo_ref[:, state_cols(i, s)] = xp[s, SUBLANES + valid - (CONV_W - 1):SUBLANES + valid, :]

    def conv_slab(t0, c0):
        xp, s = xpad_refs[c0 // PROJ_CHUNK], (c0 % PROJ_CHUNK) // LANES
        acc = None
        for j in range(CONV_W):
            r0 = t0 + SUBLANES - (CONV_W - 1) + j
            term = xp[s, r0:r0 + BLK, :] * cw_ref[j:j + 1, c0:c0 + LANES]
            acc = term if acc is None else acc + term
        return _silu(acc)

    row, col, masks = _level_masks()
    causal = row >= col
    strict = row > col
    eye = jnp.where(row == col, 1.0, 0.0).astype(F32)
    tri01 = jnp.where(causal, 1.0, 0.0).astype(BF16)

    def spatial_gating():
        gchunk = min(GMLP_CHUNK, ts)
        for c in range(ts // gchunk):
            r0 = c * gchunk
            for g in range(A_GROUPS):
                c0 = g * A_GROUP_W
                w = jnp.where(causal, ws_ref[g], 0.0).astype(BF16)
                s = _dot(w, vn_ref[r0:r0 + gchunk, c0:c0 + A_GROUP_W].astype(BF16)) + bst_ref[:, g:g + 1]
                za_ref[r0:r0 + gchunk, c0:c0 + A_GROUP_W] = (
                    ug_ref[r0:r0 + gchunk, c0:c0 + A_GROUP_W] * s).astype(BF16)

    pairs = [(c, h) for c in range(nblk) for h in range(B_HEADS)]
    per_blk = []
    for c in range(nblk):
        r0 = c * BLK
        ba = ba_ref[r0:r0 + BLK, :]
        beta_c = _sigmoid(ba)
        g_c = -jnp.exp(alog_ref[...]) * jax.nn.softplus(ba + dtb_ref[...])
        if valid < ts:
            live = lax.broadcasted_iota(jnp.int32, (BLK, BA_COLS), 0) + r0 < valid
            beta_c = jnp.where(live, beta_c, 0.0)
            g_c = jnp.where(live, g_c, 0.0)
        gam_c = sum(_dot(tri01, p) for p in _split3(g_c))
        per_blk.append((beta_c, gam_c, gam_c.T))

    def prepare(c, h):
        r0 = c * BLK
        beta_c, gam_c, gam_t = per_blk[c]
        q = conv_slab(r0, h * HEAD_COLS)
        k = conv_slab(r0, h * HEAD_COLS + B_DK)
        v = conv_slab(r0, h * HEAD_COLS + 2 * B_DK)
        q = q * lax.rsqrt(jnp.sum(q * q, axis=-1, keepdims=True) + EPS) * (B_DK ** -0.5)
        k = k * lax.rsqrt(jnp.sum(k * k, axis=-1, keepdims=True) + EPS)
        beta = beta_c[:, h:h + 1]
        gam = jnp.broadcast_to(gam_c[:, A_COL0 + h:A_COL0 + h + 1], (BLK, LANES))
        gam_row = gam_t[A_COL0 + h:A_COL0 + h + 1, :]
        gam_last = gam[BLK - 1:BLK, :]
        eg = jnp.exp(gam)
        kb = k * beta
        kq = _dot_nt(jnp.concatenate([kb.astype(BF16), q.astype(BF16)], axis=0), k.astype(BF16))
        decay = jnp.exp(jnp.where(causal, gam - gam_row, NEG_BIG))
        kgt = (k * jnp.exp(gam_last - gam)).T.astype(BF16)
        return dict(
            g=jnp.where(strict, -(kq[:BLK] * decay), 0.0),
            pk=jnp.concatenate([(kq[BLK:] * decay).astype(BF16), kgt], axis=0),
            vb=(v * beta).astype(BF16), kbg=(kb * eg).astype(BF16), qg=(q * eg).astype(BF16),
            gl=jnp.exp(gam_last))

    dense_stages = ((proj_u,), (proj_v,), (proj_z,), (proj_ga, proj_gb))
    heads_per_group = B_HEADS // len(dense_stages)
    st = {}
    for gi, stages in enumerate(dense_stages):
        for stage in stages:
            stage()
        for c in range(nblk):
            for h in range(gi * heads_per_group, (gi + 1) * heads_per_group):
                st[c, h] = prepare(c, h)
    spatial_gating()
    mix_a = sga_ref[...] * _dot(za_ref[...], pa_ref[...])

    for pr in pairs:
        d = st[pr]
        g = d["g"]
        subdiag = jnp.sum(g * masks[0], axis=0, keepdims=True)
        d["g"] = g + pltpu.roll(g, LANES - 1, axis=1) * subdiag
    for k, m in enumerate(masks[1:], start=1):
        m16 = m.astype(BF16)
        for c, h in pairs:
            if 2 ** k >= min(valid - c * BLK, BLK):
                continue
            d = st[c, h]
            g = d["g"]
            g16 = g.astype(BF16)
            d["g"] = g + _dot(g16, g16 * m16)
    for pr in pairs:
        d = st[pr]
        tinv = (eye + d["g"]).astype(BF16)
        d["u"] = _dot(tinv, d["vb"])
        d["wq"] = jnp.concatenate([_dot(tinv, d["kbg"]).astype(BF16), d["qg"]], axis=0)

    def merge(c):
        rows = slice(c * BLK, (c + 1) * BLK)
        mix = mix_a[rows] + sgb_ref[rows, :] * _dot(zb_ref[rows, :], pb_ref[...])
        y_ref[rows, :] = x_ref[rows, :] + _dot(mix.astype(BF16), wo_ref[...])

    for c in range(nblk):
        r0 = c * BLK
        cur = {}
        for h in range(B_HEADS):
            s_old = s_ref[h]
            wqs = _dot(st[c, h]["wq"], s_old.astype(BF16))
            cur[h] = (s_old, wqs[:BLK], wqs[BLK:])
        if c > 0:
            merge(c - 1)
        for h in range(B_HEADS):
            d = st[c, h]
            s_old, ws, qs = cur[h]
            pkv = _dot(d["pk"], (d["u"] - ws).astype(BF16))
            cur[h] = (s_old, qs + pkv[:BLK], pkv[BLK:])
        for h in range(B_HEADS):
            d = st[c, h]
            s_old, o, ds = cur[h]
            s_ref[h] = s_old * d["gl"] + ds
            o = o * lax.rsqrt(jnp.mean(o * o, axis=-1, keepdims=True) + EPS) * on_ref[...]
            hc = h * B_DV
            zb_ref[r0:r0 + BLK, hc:hc + B_DV] = (o * sz_ref[r0:r0 + BLK, hc:hc + B_DV]).astype(BF16)

    merge(nblk - 1)

    for xp in xpad_refs:
        xp[:, 0:SUBLANES, :] = xp[:, ts:ts + SUBLANES, :]

    @pl.when(step == pl.num_programs(1) - 1)
    def _():
        so_ref[...] = s_ref[...]


def _mixer(x, conv0, s0, consts, *, valid, want_vn):
    b, l, _ = x.shape
    ts = min(TS_MIXER, l)
    assert l % ts == 0 and ts % BLK == 0
    assert valid == ts or l == ts
    tok = lambda c: pl.BlockSpec((None, ts, c), lambda i, j: (i, j, 0))
    const = lambda a: pl.BlockSpec(a.shape, lambda i, j: (0,) * a.ndim, pipeline_mode=pl.Buffered(1))
    per_seq = lambda a: pl.BlockSpec((None,) + a.shape[1:], lambda i, j: (i,) + (0,) * (a.ndim - 1))
    tile = lambda c, dt: pltpu.VMEM((ts, c), dt)
    out_shape = [jax.ShapeDtypeStruct((b, l, D_MODEL), F32),
                 jax.ShapeDtypeStruct((b, CONV_W - 1, 3 * D_B), F32),
                 jax.ShapeDtypeStruct((b, B_HEADS, B_DK, B_DV), F32)]
    out_specs = [tok(D_MODEL),
                 pl.BlockSpec((None, CONV_W - 1, 3 * D_B), lambda i, j: (i, 0, 0)),
                 pl.BlockSpec((None, B_HEADS, B_DK, B_DV), lambda i, j: (i, 0, 0, 0))]
    scratch = [pltpu.VMEM((B_HEADS, B_DK, B_DV), F32),
               tile(D_A, F32), tile(D_B, F32), tile(BA_COLS, F32), tile(D_MODEL, F32), tile(D_MODEL, F32),
               tile(D_A, BF16), tile(D_B, BF16)]
    scratch += [pltpu.VMEM((SLABS, ts + SUBLANES, LANES), F32)] * (3 * D_B // PROJ_CHUNK)
    if want_vn:
        out_shape.append(jax.ShapeDtypeStruct((b, l, D_A), F32))
        out_specs.append(tok(D_A))
    else:
        scratch.insert(0, tile(D_A, F32))
    outs = pl.pallas_call(
        functools.partial(_mixer_kernel, ts=ts, valid=valid),
        out_shape=out_shape,
        grid=(b, l // ts),
        in_specs=[tok(D_MODEL), per_seq(conv0), per_seq(s0)] + [const(a) for a in consts],
        out_specs=out_specs,
        scratch_shapes=scratch,
        compiler_params=pltpu.CompilerParams(dimension_semantics=("arbitrary", "arbitrary"),
                                             vmem_limit_bytes=VMEM_LIMIT),
        name="mixer",
    )(x, conv0, s0, *consts)
    return tuple(outs) if want_vn else tuple(outs) + (None,)


def _ffn_kernel(x_ref, ln2_ref, wup_ref, wdn_ref, fn_ref, o_ref, *, final):
    x = x_ref[...]
    h = _rms(x, ln2_ref[...]).astype(BF16)
    acc = x
    for c in range(D_FF // FFN_CHUNK):
        c0 = c * FFN_CHUNK
        hid = _dot(h, wup_ref[:, c0:c0 + FFN_CHUNK])
        act = jnp.square(jnp.maximum(hid, 0.0)).astype(BF16)
        acc = acc + _dot(act, wdn_ref[c0:c0 + FFN_CHUNK, :])
    o_ref[...] = _rms(acc, fn_ref[...]) if final else acc


def _ffn(x2, ln2, wup, wdn, fnorm, *, final):
    t = x2.shape[0]
    tm = min(TM_FFN, t)
    row = pl.BlockSpec((tm, D_MODEL), lambda i: (i, 0))
    const = lambda a: pl.BlockSpec(a.shape, lambda i: (0, 0), pipeline_mode=pl.Buffered(1))
    return pl.pallas_call(
        functools.partial(_ffn_kernel, final=final),
        out_shape=jax.ShapeDtypeStruct((t, D_MODEL), F32),
        grid=(t // tm,),
        in_specs=[row] + [const(a) for a in (ln2, wup, wdn, fnorm)],
        out_specs=row,
        compiler_params=pltpu.CompilerParams(dimension_semantics=("arbitrary",), vmem_limit_bytes=VMEM_LIMIT),
        name="ffn",
    )(x2, ln2, wup, wdn, fnorm)


def _head_major(a):
    lead = a.shape[:-1]
    return a.reshape(*lead, 3, B_HEADS, B_DK).swapaxes(-3, -2).reshape(*lead, 3 * D_B)


def _prep_layer(l, ln1, w_in, a_ln_g, a_ln_b, w_s, b_s, conv_w, a_log, dt_bias, o_norm, p_a, p_b, w_o, ln2,
                w_up, w_down):
    w = w_in[l]
    o_qkv = 2 * D_A
    o_z = o_qkv + 3 * D_B
    o_b = o_z + D_B
    o_g = o_b + 2 * B_HEADS
    wba = jnp.zeros((D_MODEL, BA_COLS), F32).at[:, :2 * B_HEADS].set(w[:, o_b:o_g])
    pad_a = lambda v: jnp.zeros((1, BA_COLS), F32).at[0, A_COL0:A_COL0 + B_HEADS].set(v)
    row = lambda v: v.reshape(1, -1)
    mixer_consts = (
        row(ln1[l]), w[:, :o_qkv].astype(BF16), _head_major(w[:, o_qkv:o_z].astype(BF16)),
        w[:, o_z:o_b].astype(BF16),
        wba.astype(BF16), w[:, o_g:].astype(BF16), row(a_ln_g[l]), row(a_ln_b[l]),
        w_s[l], b_s[l].T, _head_major(conv_w[l].T), pad_a(a_log[l]), pad_a(dt_bias[l]), row(o_norm[l]),
        p_a[l].astype(BF16), p_b[l].astype(BF16), w_o[l].astype(BF16))
    ffn_consts = (row(ln2[l]), w_up[l].astype(BF16), w_down[l].astype(BF16))
    return mixer_consts, ffn_consts


def _layer(x, conv0, s0, consts, fnorm, *, final, want_vn):
    b, lv, _ = x.shape
    l = -(-lv // BLK) * BLK
    mixer_consts, ffn_consts = consts
    if l == lv:
        y, conv_state, s_new, vn = _mixer(x, conv0, s0, mixer_consts, valid=min(TS_MIXER, l), want_vn=want_vn)
    else:
        xp = jnp.pad(x, ((0, 0), (0, l - lv), (0, 0)))
        y, conv_state, s_new, vn = _mixer(xp, conv0, s0, mixer_consts, valid=lv, want_vn=want_vn)
        y = y[:, :lv]
        vn = vn[:, :lv] if want_vn else None
    out = _ffn(y.reshape(b * lv, D_MODEL), *ffn_consts, fnorm, final=final)
    return out.reshape(b, lv, D_MODEL), conv_state, s_new, vn


def kernel(x_prompt, x_sample, state_conv, state_delta, ln1, w_in, a_ln_g, a_ln_b, w_s, b_s, conv_w, a_log, dt_bias,
           o_norm, p_a, p_b, w_o, ln2, w_up, w_down, final_norm):
    depth = w_in.shape[0]
    bp = x_prompt.shape[0]
    yp, ys = x_prompt, x_sample
    zero_conv = jnp.zeros((bp, CONV_W - 1, 3 * D_B), F32)
    zero_delta = jnp.zeros((bp, B_HEADS, B_DK, B_DV), F32)
    fnorm = final_norm.reshape(1, -1)
    conv_p, delta_p, conv_s, delta_s, gv_s = [], [], [], [], []
    for l in range(depth):
        consts = _prep_layer(l, ln1, w_in, a_ln_g, a_ln_b, w_s, b_s, conv_w, a_log, dt_bias, o_norm, p_a, p_b, w_o,
                             ln2, w_up, w_down)
        final = l == depth - 1
        yp, cp, dp, _ = _layer(yp, zero_conv, zero_delta, consts, fnorm, final=final, want_vn=False)
        ys, cs, ds, vs = _layer(ys, state_conv[l], state_delta[l], consts, fnorm, final=final, want_vn=True)
        conv_p.append(cp); delta_p.append(dp); conv_s.append(cs); delta_s.append(ds); gv_s.append(vs)
    return (yp, ys, jnp.stack(conv_p), jnp.stack(delta_p), jnp.stack(conv_s), jnp.stack(delta_s), jnp.stack(gv_s))
```

```python
import functools

import jax
import jax.numpy as jnp
from jax import lax
from jax.experimental import pallas as pl
from jax.experimental.pallas import tpu as pltpu

F32 = jnp.float32
BF16 = jnp.bfloat16

D_MODEL = 1024
D_A = 1024
A_GROUPS = 8
A_GROUP_W = D_A // A_GROUPS
B_HEADS = 8
B_DK = 128
B_DV = 128
D_B = B_HEADS * B_DV
CONV_W = 4
D_FF = 4 * D_MODEL
EPS = 1e-6
GMLP_CHUNK = 128

LANES = 128
SUBLANES = 8
BLK = 128
BA_COLS = LANES
A_COL0 = B_HEADS
NEG_BIG = -1e30
VMEM_LIMIT = 60000 * 1024

TM_FFN = 1024
FFN_CHUNK = 1024
TS_MIXER = 256
HEAD_COLS = B_DK + B_DK + B_DV
PROJ_CHUNK = 2 * HEAD_COLS
SLABS = PROJ_CHUNK // LANES


def _rms(x, g):
    return x * lax.rsqrt(jnp.mean(x * x, axis=-1, keepdims=True) + EPS) * g


def _gelu(x):
    half = 0.5 * x
    return half + half * lax.erf(x * 0.7071067811865476)


def _sigmoid(x):
    return jax.nn.sigmoid(x)


def _silu(x):
    return x * _sigmoid(x)


def _dot(a, b):
    return jnp.dot(a, b, preferred_element_type=F32)


def _dot_nt(a, b):
    return lax.dot_general(a, b, (((1,), (1,)), ((), ())), preferred_element_type=F32)


def _split3(v):
    hi = v.astype(BF16)
    r1 = v - hi.astype(F32)
    mid = r1.astype(BF16)
    lo = (r1 - mid.astype(F32)).astype(BF16)
    return hi, mid, lo


def _level_masks():
    row = lax.broadcasted_iota(jnp.int32, (BLK, BLK), 0)
    col = lax.broadcasted_iota(jnp.int32, (BLK, BLK), 1)
    x = row ^ col
    masks = []
    b, k = 1, 0
    while b < BLK:
        masks.append(jnp.where(((x >> k) == 1) & ((row & b) != 0), 1.0, 0.0).astype(F32))
        b, k = 2 * b, k + 1
    return row, col, masks


def _mixer_kernel(x_ref, conv0_ref, s0_ref,
                  ln1_ref, wuv_ref, wqkv_ref, wz_ref, wba_ref, wg_ref, alg_ref, alb_ref,
                  ws_ref, bst_ref, cw_ref, alog_ref, dtb_ref, on_ref, pa_ref, pb_ref,
                  mix_ref, convo_ref, so_ref, *rest, ts, valid):
    vn_ref, s_ref, ug_ref, sz_ref, ba_ref, sga_ref, sgb_ref, za_ref, zb_ref, *xpad_refs = rest
    step = pl.program_id(1)
    nblk = ts // BLK
    tail = slice(SUBLANES - (CONV_W - 1), SUBLANES)

    def state_cols(i, s):
        head, part = divmod(i * SLABS + s, HEAD_COLS // LANES)
        c0 = part * D_B + head * B_DK
        return slice(c0, c0 + LANES)

    @pl.when(step == 0)
    def _():
        for i, xp in enumerate(xpad_refs):
            xp[:, 0:SUBLANES, :] = jnp.zeros((SLABS, SUBLANES, LANES), F32)
            for s in range(SLABS):
                xp[s, tail, :] = conv0_ref[:, state_cols(i, s)]
        s_ref[...] = s0_ref[...]

    xn = _rms(x_ref[...], ln1_ref[...]).astype(BF16)
    ba_ref[...] = _dot(xn, wba_ref[...])
    for i, xp in enumerate(xpad_refs):
        sec = _dot(xn, wqkv_ref[:, i * PROJ_CHUNK:(i + 1) * PROJ_CHUNK])
        for s in range(SLABS):
            xp[s, SUBLANES:SUBLANES + ts, :] = sec[:, s * LANES:(s + 1) * LANES]

    def proj_u():
        ug_ref[...] = _gelu(_dot(xn, wuv_ref[:, :D_A]))

    def proj_v():
        va = _gelu(_dot(xn, wuv_ref[:, D_A:]))
        mu = jnp.mean(va, axis=-1, keepdims=True)
        vc = va - mu
        var = jnp.mean(vc * vc, axis=-1, keepdims=True)
        vn_ref[...] = vc * lax.rsqrt(var + EPS) * alg_ref[...] + alb_ref[...]

    def proj_z():
        sz_ref[...] = _silu(_dot(xn, wz_ref[...]))

    def proj_ga():
        sga_ref[...] = _sigmoid(_dot(xn, wg_ref[:, :D_MODEL]))

    def proj_gb():
        sgb_ref[...] = _sigmoid(_dot(xn, wg_ref[:, D_MODEL:]))

    for i, xp in enumerate(xpad_refs):
        for s in range(SLABS):
            convo_ref[:, state_cols(i, s)] = xp[s, SUBLANES + valid - (CONV_W - 1):SUBLANES + valid, :]

    def conv_slab(t0, c0):
        xp, s = xpad_refs[c0 // PROJ_CHUNK], (c0 % PROJ_CHUNK) // LANES
        acc = None
        for j in range(CONV_W):
            r0 = t0 + SUBLANES - (CONV_W - 1) + j
            term = xp[s, r0:r0 + BLK, :] * cw_ref[j:j + 1, c0:c0 + LANES]
            acc = term if acc is None else acc + term
        return _silu(acc)

    row, col, masks = _level_masks()
    causal = row >= col
    strict = row > col
    eye = jnp.where(row == col, 1.0, 0.0).astype(F32)
    tri01 = jnp.where(causal, 1.0, 0.0).astype(BF16)

    def spatial_gating():
        gchunk = min(GMLP_CHUNK, ts)
        for c in range(ts // gchunk):
            r0 = c * gchunk
            for g in range(A_GROUPS):
                c0 = g * A_GROUP_W
                w = jnp.where(causal, ws_ref[g], 0.0).astype(BF16)
                s = _dot(w, vn_ref[r0:r0 + gchunk, c0:c0 + A_GROUP_W].astype(BF16)) + bst_ref[:, g:g + 1]
                za_ref[r0:r0 + gchunk, c0:c0 + A_GROUP_W] = (
                    ug_ref[r0:r0 + gchunk, c0:c0 + A_GROUP_W] * s).astype(BF16)

    pairs = [(c, h) for c in range(nblk) for h in range(B_HEADS)]
    per_blk = []
    for c in range(nblk):
        r0 = c * BLK
        ba = ba_ref[r0:r0 + BLK, :]
        beta_c = _sigmoid(ba)
        g_c = -jnp.exp(alog_ref[...]) * jax.nn.softplus(ba + dtb_ref[...])
        if valid < ts:
            live = lax.broadcasted_iota(jnp.int32, (BLK, BA_COLS), 0) + r0 < valid
            beta_c = jnp.where(live, beta_c, 0.0)
            g_c = jnp.where(live, g_c, 0.0)
        gam_c = sum(_dot(tri01, p) for p in _split3(g_c))
        per_blk.append((beta_c, gam_c, gam_c.T))

    def prepare(c, h):
        r0 = c * BLK
        beta_c, gam_c, gam_t = per_blk[c]
        q = conv_slab(r0, h * HEAD_COLS)
        k = conv_slab(r0, h * HEAD_COLS + B_DK)
        v = conv_slab(r0, h * HEAD_COLS + 2 * B_DK)
        q = q * lax.rsqrt(jnp.sum(q * q, axis=-1, keepdims=True) + EPS) * (B_DK ** -0.5)
        k = k * lax.rsqrt(jnp.sum(k * k, axis=-1, keepdims=True) + EPS)
        beta = beta_c[:, h:h + 1]
        gam = jnp.broadcast_to(gam_c[:, A_COL0 + h:A_COL0 + h + 1], (BLK, LANES))
        gam_row = gam_t[A_COL0 + h:A_COL0 + h + 1, :]
        gam_last = gam[BLK - 1:BLK, :]
        eg = jnp.exp(gam)
        kb = k * beta
        kq = _dot_nt(jnp.concatenate([kb.astype(BF16), q.astype(BF16)], axis=0), k.astype(BF16))
        decay = jnp.exp(jnp.where(causal, gam - gam_row, NEG_BIG))
        kgt = (k * jnp.exp(gam_last - gam)).T.astype(BF16)
        return dict(
            g=jnp.where(strict, -(kq[:BLK] * decay), 0.0),
            pk=jnp.concatenate([(kq[BLK:] * decay).astype(BF16), kgt], axis=0),
            vb=(v * beta).astype(BF16), kbg=(kb * eg).astype(BF16), qg=(q * eg).astype(BF16),
            gl=jnp.exp(gam_last))

    dense_stages = ((proj_u,), (proj_v,), (proj_z,), (proj_ga, proj_gb))
    heads_per_group = B_HEADS // len(dense_stages)
    st = {}
    for gi, stages in enumerate(dense_stages):
        for stage in stages:
            stage()
        for c in range(nblk):
            for h in range(gi * heads_per_group, (gi + 1) * heads_per_group):
                st[c, h] = prepare(c, h)
    spatial_gating()
    mix_a = sga_ref[...] * _dot(za_ref[...], pa_ref[...])

    for pr in pairs:
        d = st[pr]
        g = d["g"]
        subdiag = jnp.sum(g * masks[0], axis=0, keepdims=True)
        d["g"] = g + pltpu.roll(g, LANES - 1, axis=1) * subdiag
    for k, m in enumerate(masks[1:], start=1):
        m16 = m.astype(BF16)
        for c, h in pairs:
            if 2 ** k >= min(valid - c * BLK, BLK):
                continue
            d = st[c, h]
            g = d["g"]
            g16 = g.astype(BF16)
            d["g"] = g + _dot(g16, g16 * m16)
    for pr in pairs:
        d = st[pr]
        tinv = (eye + d["g"]).astype(BF16)
        d["u"] = _dot(tinv, d["vb"])
        d["wq"] = jnp.concatenate([_dot(tinv, d["kbg"]).astype(BF16), d["qg"]], axis=0)

    def merge(c):
        rows = slice(c * BLK, (c + 1) * BLK)
        mix = mix_a[rows] + sgb_ref[rows, :] * _dot(zb_ref[rows, :], pb_ref[...])
        mix_ref[rows, :] = mix.astype(BF16)

    for c in range(nblk):
        r0 = c * BLK
        cur = {}
        for h in range(B_HEADS):
            s_old = s_ref[h]
            wqs = _dot(st[c, h]["wq"], s_old.astype(BF16))
            cur[h] = (s_old, wqs[:BLK], wqs[BLK:])
        if c > 0:
            merge(c - 1)
        for h in range(B_HEADS):
            d = st[c, h]
            s_old, ws, qs = cur[h]
            pkv = _dot(d["pk"], (d["u"] - ws).astype(BF16))
            cur[h] = (s_old, qs + pkv[:BLK], pkv[BLK:])
        for h in range(B_HEADS):
            d = st[c, h]
            s_old, o, ds = cur[h]
            s_ref[h] = s_old * d["gl"] + ds
            o = o * lax.rsqrt(jnp.mean(o * o, axis=-1, keepdims=True) + EPS) * on_ref[...]
            hc = h * B_DV
            zb_ref[r0:r0 + BLK, hc:hc + B_DV] = (o * sz_ref[r0:r0 + BLK, hc:hc + B_DV]).astype(BF16)

    merge(nblk - 1)

    for xp in xpad_refs:
        xp[:, 0:SUBLANES, :] = xp[:, ts:ts + SUBLANES, :]

    @pl.when(step == pl.num_programs(1) - 1)
    def _():
        so_ref[...] = s_ref[...]


def _mixer(x, conv0, s0, consts, *, valid, want_vn):
    b, l, _ = x.shape
    ts = min(TS_MIXER, l)
    assert l % ts == 0 and ts % BLK == 0
    assert valid == ts or l == ts
    tok = lambda c: pl.BlockSpec((None, ts, c), lambda i, j: (i, j, 0))
    const = lambda a: pl.BlockSpec(a.shape, lambda i, j: (0,) * a.ndim, pipeline_mode=pl.Buffered(1))
    per_seq = lambda a: pl.BlockSpec((None,) + a.shape[1:], lambda i, j: (i,) + (0,) * (a.ndim - 1))
    tile = lambda c, dt: pltpu.VMEM((ts, c), dt)
    out_shape = [jax.ShapeDtypeStruct((b, l, D_MODEL), BF16),
                 jax.ShapeDtypeStruct((b, CONV_W - 1, 3 * D_B), F32),
                 jax.ShapeDtypeStruct((b, B_HEADS, B_DK, B_DV), F32)]
    out_specs = [tok(D_MODEL),
                 pl.BlockSpec((None, CONV_W - 1, 3 * D_B), lambda i, j: (i, 0, 0)),
                 pl.BlockSpec((None, B_HEADS, B_DK, B_DV), lambda i, j: (i, 0, 0, 0))]
    scratch = [pltpu.VMEM((B_HEADS, B_DK, B_DV), F32),
               tile(D_A, F32), tile(D_B, F32), tile(BA_COLS, F32), tile(D_MODEL, F32), tile(D_MODEL, F32),
               tile(D_A, BF16), tile(D_B, BF16)]
    scratch += [pltpu.VMEM((SLABS, ts + SUBLANES, LANES), F32)] * (3 * D_B // PROJ_CHUNK)
    if want_vn:
        out_shape.append(jax.ShapeDtypeStruct((b, l, D_A), F32))
        out_specs.append(tok(D_A))
    else:
        scratch.insert(0, tile(D_A, F32))
    outs = pl.pallas_call(
        functools.partial(_mixer_kernel, ts=ts, valid=valid),
        out_shape=out_shape,
        grid=(b, l // ts),
        in_specs=[tok(D_MODEL), per_seq(conv0), per_seq(s0)] + [const(a) for a in consts],
        out_specs=out_specs,
        scratch_shapes=scratch,
        compiler_params=pltpu.CompilerParams(dimension_semantics=("arbitrary", "arbitrary"),
                                             vmem_limit_bytes=VMEM_LIMIT),
        name="mixer",
    )(x, conv0, s0, *consts)
    return tuple(outs) if want_vn else tuple(outs) + (None,)


def _ffn_kernel(x_ref, mix_ref, wo_ref, ln2_ref, wup_ref, wdn_ref, fn_ref, o_ref, *, final):
    x = x_ref[...] + _dot(mix_ref[...], wo_ref[...])
    h = _rms(x, ln2_ref[...]).astype(BF16)
    acc = x
    for c in range(D_FF // FFN_CHUNK):
        c0 = c * FFN_CHUNK
        hid = _dot(h, wup_ref[:, c0:c0 + FFN_CHUNK])
        act = jnp.square(jnp.maximum(hid, 0.0)).astype(BF16)
        acc = acc + _dot(act, wdn_ref[c0:c0 + FFN_CHUNK, :])
    o_ref[...] = _rms(acc, fn_ref[...]) if final else acc


def _ffn(x2, mix2, wo, ln2, wup, wdn, fnorm, *, final):
    t = x2.shape[0]
    tm = min(TM_FFN, t)
    row = pl.BlockSpec((tm, D_MODEL), lambda i: (i, 0))
    const = lambda a: pl.BlockSpec(a.shape, lambda i: (0, 0), pipeline_mode=pl.Buffered(1))
    return pl.pallas_call(
        functools.partial(_ffn_kernel, final=final),
        out_shape=jax.ShapeDtypeStruct((t, D_MODEL), F32),
        grid=(t // tm,),
        in_specs=[row, row] + [const(a) for a in (wo, ln2, wup, wdn, fnorm)],
        out_specs=row,
        compiler_params=pltpu.CompilerParams(dimension_semantics=("arbitrary",), vmem_limit_bytes=VMEM_LIMIT),
        name="ffn",
    )(x2, mix2, wo, ln2, wup, wdn, fnorm)


def _head_major(a):
    lead = a.shape[:-1]
    return a.reshape(*lead, 3, B_HEADS, B_DK).swapaxes(-3, -2).reshape(*lead, 3 * D_B)


def _prep_layer(l, ln1, w_in, a_ln_g, a_ln_b, w_s, b_s, conv_w, a_log, dt_bias, o_norm, p_a, p_b, w_o, ln2,
                w_up, w_down):
    w = w_in[l]
    o_qkv = 2 * D_A
    o_z = o_qkv + 3 * D_B
    o_b = o_z + D_B
    o_g = o_b + 2 * B_HEADS
    wba = jnp.zeros((D_MODEL, BA_COLS), F32).at[:, :2 * B_HEADS].set(w[:, o_b:o_g])
    pad_a = lambda v: jnp.zeros((1, BA_COLS), F32).at[0, A_COL0:A_COL0 + B_HEADS].set(v)
    row = lambda v: v.reshape(1, -1)
    mixer_consts = (
        row(ln1[l]), w[:, :o_qkv].astype(BF16), _head_major(w[:, o_qkv:o_z].astype(BF16)),
        w[:, o_z:o_b].astype(BF16),
        wba.astype(BF16), w[:, o_g:].astype(BF16), row(a_ln_g[l]), row(a_ln_b[l]),
        w_s[l], b_s[l].T, _head_major(conv_w[l].T), pad_a(a_log[l]), pad_a(dt_bias[l]), row(o_norm[l]),
        p_a[l].astype(BF16), p_b[l].astype(BF16))
    ffn_consts = (w_o[l].astype(BF16), row(ln2[l]), w_up[l].astype(BF16), w_down[l].astype(BF16))
    return mixer_consts, ffn_consts


def _layer(x, conv0, s0, consts, fnorm, *, final, want_vn):
    b, lv, _ = x.shape
    l = -(-lv // BLK) * BLK
    mixer_consts, ffn_consts = consts
    if l == lv:
        mix, conv_state, s_new, vn = _mixer(x, conv0, s0, mixer_consts, valid=min(TS_MIXER, l), want_vn=want_vn)
    else:
        xp = jnp.pad(x, ((0, 0), (0, l - lv), (0, 0)))
        mix, conv_state, s_new, vn = _mixer(xp, conv0, s0, mixer_consts, valid=lv, want_vn=want_vn)
        mix = mix[:, :lv]
        vn = vn[:, :lv] if want_vn else None
    out = _ffn(x.reshape(b * lv, D_MODEL), mix.reshape(b * lv, D_MODEL), *ffn_consts, fnorm, final=final)
    return out.reshape(b, lv, D_MODEL), conv_state, s_new, vn


def kernel(x_prompt, x_sample, state_conv, state_delta, ln1, w_in, a_ln_g, a_ln_b, w_s, b_s, conv_w, a_log, dt_bias,
           o_norm, p_a, p_b, w_o, ln2, w_up, w_down, final_norm):
    depth = w_in.shape[0]
    bp = x_prompt.shape[0]
    yp, ys = x_prompt, x_sample
    zero_conv = jnp.zeros((bp, CONV_W - 1, 3 * D_B), F32)
    zero_delta = jnp.zeros((bp, B_HEADS, B_DK, B_DV), F32)
    fnorm = final_norm.reshape(1, -1)
    conv_p, delta_p, conv_s, delta_s, gv_s = [], [], [], [], []
    for l in range(depth):
        consts = _prep_layer(l, ln1, w_in, a_ln_g, a_ln_b, w_s, b_s, conv_w, a_log, dt_bias, o_norm, p_a, p_b, w_o,
                             ln2, w_up, w_down)
        final = l == depth - 1
        yp, cp, dp, _ = _layer(yp, zero_conv, zero_delta, consts, fnorm, final=final, want_vn=False)
        ys, cs, ds, vs = _layer(ys, state_conv[l], state_delta[l], consts, fnorm, final=final, want_vn=True)
        conv_p.append(cp); delta_p.append(dp); conv_s.append(cs); delta_s.append(ds); gv_s.append(vs)
    return (yp, ys, jnp.stack(conv_p), jnp.stack(delta_p), jnp.stack(conv_s), jnp.stack(delta_s), jnp.stack(gv_s))
```

```python
import functools

import jax
import jax.numpy as jnp
from jax import lax
from jax.experimental import pallas as pl
from jax.experimental.pallas import tpu as pltpu

F32 = jnp.float32
BF16 = jnp.bfloat16

D_MODEL = 1024
D_A = 1024
A_GROUPS = 8
A_GROUP_W = D_A // A_GROUPS
B_HEADS = 8
B_DK = 128
B_DV = 128
D_B = B_HEADS * B_DV
CONV_W = 4
D_FF = 4 * D_MODEL
EPS = 1e-6
GMLP_CHUNK = 128

LANES = 128
SUBLANES = 8
BLK = 128
BA_COLS = LANES
A_COL0 = B_HEADS
NEG_BIG = -1e30
VMEM_LIMIT = 60000 * 1024

TM_FFN = 512
FFN_CHUNK = 1024
TS_MIXER = 256
HEAD_COLS = B_DK + B_DK + B_DV
PROJ_CHUNK = 2 * HEAD_COLS
SLABS = PROJ_CHUNK // LANES


def _rms(x, g):
    return x * lax.rsqrt(jnp.mean(x * x, axis=-1, keepdims=True) + EPS) * g


def _gelu(x):
    half = 0.5 * x
    return half + half * lax.erf(x * 0.7071067811865476)


def _sigmoid(x):
    return jax.nn.sigmoid(x)


def _silu(x):
    return x * _sigmoid(x)


def _dot(a, b):
    return jnp.dot(a, b, preferred_element_type=F32)


def _dot_nt(a, b):
    return lax.dot_general(a, b, (((1,), (1,)), ((), ())), preferred_element_type=F32)


def _split3(v):
    hi = v.astype(BF16)
    r1 = v - hi.astype(F32)
    mid = r1.astype(BF16)
    lo = (r1 - mid.astype(F32)).astype(BF16)
    return hi, mid, lo


def _level_masks():
    row = lax.broadcasted_iota(jnp.int32, (BLK, BLK), 0)
    col = lax.broadcasted_iota(jnp.int32, (BLK, BLK), 1)
    x = row ^ col
    masks = []
    b, k = 1, 0
    while b < BLK:
        masks.append(jnp.where(((x >> k) == 1) & ((row & b) != 0), 1.0, 0.0).astype(F32))
        b, k = 2 * b, k + 1
    return row, col, masks


def _mixer_kernel(x_ref, conv0_ref, s0_ref,
                  ln1_ref, wuv_ref, wqkv_ref, wz_ref, wba_ref, wg_ref, alg_ref, alb_ref,
                  ws_ref, bst_ref, cw_ref, alog_ref, dtb_ref, on_ref, pa_ref,
                  mixa_ref, gate_ref, zbo_ref, convo_ref, so_ref, *rest, ts, valid):
    vn_ref, s_ref, ug_ref, sz_ref, ba_ref, sga_ref, sgb_ref, za_ref, zb_ref, *xpad_refs = rest
    step = pl.program_id(1)
    nblk = ts // BLK
    tail = slice(SUBLANES - (CONV_W - 1), SUBLANES)

    def state_cols(i, s):
        head, part = divmod(i * SLABS + s, HEAD_COLS // LANES)
        c0 = part * D_B + head * B_DK
        return slice(c0, c0 + LANES)

    @pl.when(step == 0)
    def _():
        for i, xp in enumerate(xpad_refs):
            xp[:, 0:SUBLANES, :] = jnp.zeros((SLABS, SUBLANES, LANES), F32)
            for s in range(SLABS):
                xp[s, tail, :] = conv0_ref[:, state_cols(i, s)]
        s_ref[...] = s0_ref[...]

    xn = _rms(x_ref[...], ln1_ref[...]).astype(BF16)
    ba_ref[...] = _dot(xn, wba_ref[...])
    for i, xp in enumerate(xpad_refs):
        sec = _dot(xn, wqkv_ref[:, i * PROJ_CHUNK:(i + 1) * PROJ_CHUNK])
        for s in range(SLABS):
            xp[s, SUBLANES:SUBLANES + ts, :] = sec[:, s * LANES:(s + 1) * LANES]

    def proj_u():
        ug_ref[...] = _gelu(_dot(xn, wuv_ref[:, :D_A]))

    def proj_v():
        va = _gelu(_dot(xn, wuv_ref[:, D_A:]))
        mu = jnp.mean(va, axis=-1, keepdims=True)
        vc = va - mu
        var = jnp.mean(vc * vc, axis=-1, keepdims=True)
        vn_ref[...] = vc * lax.rsqrt(var + EPS) * alg_ref[...] + alb_ref[...]

    def proj_z():
        sz_ref[...] = _silu(_dot(xn, wz_ref[...]))

    def proj_ga():
        sga_ref[...] = _sigmoid(_dot(xn, wg_ref[:, :D_MODEL]))

    def proj_gb():
        sgb_ref[...] = _sigmoid(_dot(xn, wg_ref[:, D_MODEL:]))

    for i, xp in enumerate(xpad_refs):
        for s in range(SLABS):
            convo_ref[:, state_cols(i, s)] = xp[s, SUBLANES + valid - (CONV_W - 1):SUBLANES + valid, :]

    def conv_slab(t0, c0):
        xp, s = xpad_refs[c0 // PROJ_CHUNK], (c0 % PROJ_CHUNK) // LANES
        acc = None
        for j in range(CONV_W):
            r0 = t0 + SUBLANES - (CONV_W - 1) + j
            term = xp[s, r0:r0 + BLK, :] * cw_ref[j:j + 1, c0:c0 + LANES]
            acc = term if acc is None else acc + term
        return _silu(acc)

    row, col, masks = _level_masks()
    causal = row >= col
    strict = row > col
    eye = jnp.where(row == col, 1.0, 0.0).astype(F32)
    tri01 = jnp.where(causal, 1.0, 0.0).astype(BF16)

    def spatial_gating():
        gchunk = min(GMLP_CHUNK, ts)
        for c in range(ts // gchunk):
            r0 = c * gchunk
            for g in range(A_GROUPS):
                c0 = g * A_GROUP_W
                w = jnp.where(causal, ws_ref[g], 0.0).astype(BF16)
                s = _dot(w, vn_ref[r0:r0 + gchunk, c0:c0 + A_GROUP_W].astype(BF16)) + bst_ref[:, g:g + 1]
                za_ref[r0:r0 + gchunk, c0:c0 + A_GROUP_W] = (
                    ug_ref[r0:r0 + gchunk, c0:c0 + A_GROUP_W] * s).astype(BF16)

    pairs = [(c, h) for c in range(nblk) for h in range(B_HEADS)]
    per_blk = []
    for c in range(nblk):
        r0 = c * BLK
        ba = ba_ref[r0:r0 + BLK, :]
        beta_c = _sigmoid(ba)
        g_c = -jnp.exp(alog_ref[...]) * jax.nn.softplus(ba + dtb_ref[...])
        if valid < ts:
            live = lax.broadcasted_iota(jnp.int32, (BLK, BA_COLS), 0) + r0 < valid
            beta_c = jnp.where(live, beta_c, 0.0)
            g_c = jnp.where(live, g_c, 0.0)
        gam_c = sum(_dot(tri01, p) for p in _split3(g_c))
        per_blk.append((beta_c, gam_c, gam_c.T))

    def prepare(c, h):
        r0 = c * BLK
        beta_c, gam_c, gam_t = per_blk[c]
        q = conv_slab(r0, h * HEAD_COLS)
        k = conv_slab(r0, h * HEAD_COLS + B_DK)
        v = conv_slab(r0, h * HEAD_COLS + 2 * B_DK)
        q = q * lax.rsqrt(jnp.sum(q * q, axis=-1, keepdims=True) + EPS) * (B_DK ** -0.5)
        k = k * lax.rsqrt(jnp.sum(k * k, axis=-1, keepdims=True) + EPS)
        beta = beta_c[:, h:h + 1]
        gam = jnp.broadcast_to(gam_c[:, A_COL0 + h:A_COL0 + h + 1], (BLK, LANES))
        gam_row = gam_t[A_COL0 + h:A_COL0 + h + 1, :]
        gam_last = gam[BLK - 1:BLK, :]
        eg = jnp.exp(gam)
        kb = k * beta
        kq = _dot_nt(jnp.concatenate([kb.astype(BF16), q.astype(BF16)], axis=0), k.astype(BF16))
        decay = jnp.exp(jnp.where(causal, gam - gam_row, NEG_BIG))
        kgt = (k * jnp.exp(gam_last - gam)).T.astype(BF16)
        return dict(
            g=jnp.where(strict, -(kq[:BLK] * decay), 0.0),
            pk=jnp.concatenate([(kq[BLK:] * decay).astype(BF16), kgt], axis=0),
            vb=(v * beta).astype(BF16), kbg=(kb * eg).astype(BF16), qg=(q * eg).astype(BF16),
            gl=jnp.exp(gam_last))

    dense_stages = ((proj_u,), (proj_v,), (proj_z,), (proj_ga, proj_gb))
    heads_per_group = B_HEADS // len(dense_stages)
    st = {}
    for gi, stages in enumerate(dense_stages):
        for stage in stages:
            stage()
        for c in range(nblk):
            for h in range(gi * heads_per_group, (gi + 1) * heads_per_group):
                st[c, h] = prepare(c, h)
    spatial_gating()
    mix_a = sga_ref[...] * _dot(za_ref[...], pa_ref[...])

    for pr in pairs:
        d = st[pr]
        g = d["g"]
        subdiag = jnp.sum(g * masks[0], axis=0, keepdims=True)
        d["g"] = g + pltpu.roll(g, LANES - 1, axis=1) * subdiag
    for k, m in enumerate(masks[1:], start=1):
        m16 = m.astype(BF16)
        for c, h in pairs:
            if 2 ** k >= min(valid - c * BLK, BLK):
                continue
            d = st[c, h]
            g = d["g"]
            g16 = g.astype(BF16)
            d["g"] = g + _dot(g16, g16 * m16)
    for pr in pairs:
        d = st[pr]
        tinv = (eye + d["g"]).astype(BF16)
        d["u"] = _dot(tinv, d["vb"])
        d["wq"] = jnp.concatenate([_dot(tinv, d["kbg"]).astype(BF16), d["qg"]], axis=0)

    def merge(c):
        rows = slice(c * BLK, (c + 1) * BLK)
        mixa_ref[rows, :] = mix_a[rows]
        gate_ref[rows, :] = sgb_ref[rows, :]
        zbo_ref[rows, :] = zb_ref[rows, :]

    for c in range(nblk):
        r0 = c * BLK
        cur = {}
        for h in range(B_HEADS):
            s_old = s_ref[h]
            wqs = _dot(st[c, h]["wq"], s_old.astype(BF16))
            cur[h] = (s_old, wqs[:BLK], wqs[BLK:])
        if c > 0:
            merge(c - 1)
        for h in range(B_HEADS):
            d = st[c, h]
            s_old, ws, qs = cur[h]
            pkv = _dot(d["pk"], (d["u"] - ws).astype(BF16))
            cur[h] = (s_old, qs + pkv[:BLK], pkv[BLK:])
        for h in range(B_HEADS):
            d = st[c, h]
            s_old, o, ds = cur[h]
            s_ref[h] = s_old * d["gl"] + ds
            o = o * lax.rsqrt(jnp.mean(o * o, axis=-1, keepdims=True) + EPS) * on_ref[...]
            hc = h * B_DV
            zb_ref[r0:r0 + BLK, hc:hc + B_DV] = (o * sz_ref[r0:r0 + BLK, hc:hc + B_DV]).astype(BF16)

    merge(nblk - 1)

    for xp in xpad_refs:
        xp[:, 0:SUBLANES, :] = xp[:, ts:ts + SUBLANES, :]

    @pl.when(step == pl.num_programs(1) - 1)
    def _():
        so_ref[...] = s_ref[...]


def _mixer(x, conv0, s0, consts, *, valid, want_vn):
    b, l, _ = x.shape
    ts = min(TS_MIXER, l)
    assert l % ts == 0 and ts % BLK == 0
    assert valid == ts or l == ts
    tok = lambda c: pl.BlockSpec((None, ts, c), lambda i, j: (i, j, 0))
    const = lambda a: pl.BlockSpec(a.shape, lambda i, j: (0,) * a.ndim, pipeline_mode=pl.Buffered(1))
    per_seq = lambda a: pl.BlockSpec((None,) + a.shape[1:], lambda i, j: (i,) + (0,) * (a.ndim - 1))
    tile = lambda c, dt: pltpu.VMEM((ts, c), dt)
    out_shape = [jax.ShapeDtypeStruct((b, l, D_MODEL), BF16),
                 jax.ShapeDtypeStruct((b, CONV_W - 1, 3 * D_B), F32),
                 jax.ShapeDtypeStruct((b, B_HEADS, B_DK, B_DV), F32)]
    out_shape = [jax.ShapeDtypeStruct((b, l, D_MODEL), F32), jax.ShapeDtypeStruct((b, l, D_MODEL), F32),
                 jax.ShapeDtypeStruct((b, l, D_B), BF16)] + out_shape[1:]
    out_specs = [tok(D_MODEL), tok(D_MODEL), tok(D_B),
                 pl.BlockSpec((None, CONV_W - 1, 3 * D_B), lambda i, j: (i, 0, 0)),
                 pl.BlockSpec((None, B_HEADS, B_DK, B_DV), lambda i, j: (i, 0, 0, 0))]
    scratch = [pltpu.VMEM((B_HEADS, B_DK, B_DV), F32),
               tile(D_A, F32), tile(D_B, F32), tile(BA_COLS, F32), tile(D_MODEL, F32), tile(D_MODEL, F32),
               tile(D_A, BF16), tile(D_B, BF16)]
    scratch += [pltpu.VMEM((SLABS, ts + SUBLANES, LANES), F32)] * (3 * D_B // PROJ_CHUNK)
    if want_vn:
        out_shape.append(jax.ShapeDtypeStruct((b, l, D_A), F32))
        out_specs.append(tok(D_A))
    else:
        scratch.insert(0, tile(D_A, F32))
    outs = pl.pallas_call(
        functools.partial(_mixer_kernel, ts=ts, valid=valid),
        out_shape=out_shape,
        grid=(b, l // ts),
        in_specs=[tok(D_MODEL), per_seq(conv0), per_seq(s0)] + [const(a) for a in consts],
        out_specs=out_specs,
        scratch_shapes=scratch,
        compiler_params=pltpu.CompilerParams(dimension_semantics=("arbitrary", "arbitrary"),
                                             vmem_limit_bytes=VMEM_LIMIT),
        name="mixer",
    )(x, conv0, s0, *consts)
    return tuple(outs) if want_vn else tuple(outs) + (None,)


def _ffn_kernel(x_ref, mixa_ref, gate_ref, zb_ref, pb_ref, wo_ref, ln2_ref, wup_ref, wdn_ref, fn_ref, o_ref, *,
                final):
    mix = mixa_ref[...] + gate_ref[...] * _dot(zb_ref[...], pb_ref[...])
    x = x_ref[...] + _dot(mix.astype(BF16), wo_ref[...])
    h = _rms(x, ln2_ref[...]).astype(BF16)
    acc = x
    for c in range(D_FF // FFN_CHUNK):
        c0 = c * FFN_CHUNK
        hid = _dot(h, wup_ref[:, c0:c0 + FFN_CHUNK])
        act = jnp.square(jnp.maximum(hid, 0.0)).astype(BF16)
        acc = acc + _dot(act, wdn_ref[c0:c0 + FFN_CHUNK, :])
    o_ref[...] = _rms(acc, fn_ref[...]) if final else acc


def _ffn(x2, mixa2, gate2, zb2, pb, wo, ln2, wup, wdn, fnorm, *, final):
    t = x2.shape[0]
    tm = min(TM_FFN, t)
    row = pl.BlockSpec((tm, D_MODEL), lambda i: (i, 0))
    const = lambda a: pl.BlockSpec(a.shape, lambda i: (0, 0), pipeline_mode=pl.Buffered(1))
    return pl.pallas_call(
        functools.partial(_ffn_kernel, final=final),
        out_shape=jax.ShapeDtypeStruct((t, D_MODEL), F32),
        grid=(t // tm,),
        in_specs=[row, row, row, row] + [const(a) for a in (pb, wo, ln2, wup, wdn, fnorm)],
        out_specs=row,
        compiler_params=pltpu.CompilerParams(dimension_semantics=("arbitrary",), vmem_limit_bytes=VMEM_LIMIT),
        name="ffn",
    )(x2, mixa2, gate2, zb2, pb, wo, ln2, wup, wdn, fnorm)


def _head_major(a):
    lead = a.shape[:-1]
    return a.reshape(*lead, 3, B_HEADS, B_DK).swapaxes(-3, -2).reshape(*lead, 3 * D_B)


def _prep_layer(l, ln1, w_in, a_ln_g, a_ln_b, w_s, b_s, conv_w, a_log, dt_bias, o_norm, p_a, p_b, w_o, ln2,
                w_up, w_down):
    w = w_in[l]
    o_qkv = 2 * D_A
    o_z = o_qkv + 3 * D_B
    o_b = o_z + D_B
    o_g = o_b + 2 * B_HEADS
    wba = jnp.zeros((D_MODEL, BA_COLS), F32).at[:, :2 * B_HEADS].set(w[:, o_b:o_g])
    pad_a = lambda v: jnp.zeros((1, BA_COLS), F32).at[0, A_COL0:A_COL0 + B_HEADS].set(v)
    row = lambda v: v.reshape(1, -1)
    mixer_consts = (
        row(ln1[l]), w[:, :o_qkv].astype(BF16), _head_major(w[:, o_qkv:o_z].astype(BF16)),
        w[:, o_z:o_b].astype(BF16),
        wba.astype(BF16), w[:, o_g:].astype(BF16), row(a_ln_g[l]), row(a_ln_b[l]),
        w_s[l], b_s[l].T, _head_major(conv_w[l].T), pad_a(a_log[l]), pad_a(dt_bias[l]), row(o_norm[l]),
        p_a[l].astype(BF16))
    ffn_consts = (p_b[l].astype(BF16), w_o[l].astype(BF16), row(ln2[l]), w_up[l].astype(BF16),
                  w_down[l].astype(BF16))
    return mixer_consts, ffn_consts


def _layer(x, conv0, s0, consts, fnorm, *, final, want_vn):
    b, lv, _ = x.shape
    l = -(-lv // BLK) * BLK
    mixer_consts, ffn_consts = consts
    if l == lv:
        *parts, conv_state, s_new, vn = _mixer(x, conv0, s0, mixer_consts, valid=min(TS_MIXER, l), want_vn=want_vn)
    else:
        xp = jnp.pad(x, ((0, 0), (0, l - lv), (0, 0)))
        *parts, conv_state, s_new, vn = _mixer(xp, conv0, s0, mixer_consts, valid=lv, want_vn=want_vn)
        parts = [a[:, :lv] for a in parts]
        vn = vn[:, :lv] if want_vn else None
    rows = lambda a: a.reshape(b * lv, a.shape[-1])
    out = _ffn(rows(x), *[rows(a) for a in parts], *ffn_consts, fnorm, final=final)
    return out.reshape(b, lv, D_MODEL), conv_state, s_new, vn


def kernel(x_prompt, x_sample, state_conv, state_delta, ln1, w_in, a_ln_g, a_ln_b, w_s, b_s, conv_w, a_log, dt_bias,
           o_norm, p_a, p_b, w_o, ln2, w_up, w_down, final_norm):
    depth = w_in.shape[0]
    bp = x_prompt.shape[0]
    yp, ys = x_prompt, x_sample
    zero_conv = jnp.zeros((bp, CONV_W - 1, 3 * D_B), F32)
    zero_delta = jnp.zeros((bp, B_HEADS, B_DK, B_DV), F32)
    fnorm = final_norm.reshape(1, -1)
    conv_p, delta_p, conv_s, delta_s, gv_s = [], [], [], [], []
    for l in range(depth):
        consts = _prep_layer(l, ln1, w_in, a_ln_g, a_ln_b, w_s, b_s, conv_w, a_log, dt_bias, o_norm, p_a, p_b, w_o,
                             ln2, w_up, w_down)
        final = l == depth - 1
        yp, cp, dp, _ = _layer(yp, zero_conv, zero_delta, consts, fnorm, final=final, want_vn=False)
        ys, cs, ds, vs = _layer(ys, state_conv[l], state_delta[l], consts, fnorm, final=final, want_vn=True)
        conv_p.append(cp); delta_p.append(dp); conv_s.append(cs); delta_s.append(ds); gv_s.append(vs)
    return (yp, ys, jnp.stack(conv_p), jnp.stack(delta_p), jnp.stack(conv_s), jnp.stack(delta_s), jnp.stack(gv_s))
```
